```python
import functools
import jax
import jax.numpy as jnp
from jax import lax
import numpy as np

D_MODEL = 4096
BATCH = 4
SEQ = 2048
DEPTH = 1
DEC_BATCH = 32
DEC_SEQ = 1
PAST_LEN = 8192
PAGE_SIZE = 128

NORM_EPS = 1e-5
RW_WIDTH = D_MODEL // 2
RW_HEAD = 64
RW_HEADS = RW_WIDTH // RW_HEAD
RW_DECAY_LORA = max(32, int(round(1.8 * RW_WIDTH ** 0.5 / 32)) * 32)
RW_AAA_LORA = RW_DECAY_LORA
RW_GATE_LORA = max(32, int(round(0.6 * RW_WIDTH ** 0.8 / 32)) * 32)
RW_GN_EPS = 64e-5
RW_COLS = 3 * RW_WIDTH + RW_DECAY_LORA + RW_AAA_LORA + RW_GATE_LORA
MB_WIDTH = D_MODEL // 2
MB_HEAD = 128
MB_HEADS = MB_WIDTH // MB_HEAD
MB_BLOCK = 256
MB_TOPK = 3
MB_Q_CHUNK = 16
MB_COLS = 3 * MB_WIDTH
MB_SCALE = MB_HEAD ** -0.5
ROPE_DIM = MB_HEAD // 4
ROPE_THETA = 500000.0
GATE_COLS = 2 * D_MODEL
N_IN = RW_COLS + MB_COLS + GATE_COLS
N_MEM = 256
X_HEADS = 4
X_HEAD = 128
X_WIDTH = X_HEADS * X_HEAD
D_FF = ((8 * D_MODEL // 3 + 255) // 256) * 256
CONV_W = 3

kernel_name = 'rwkv7_moba_hybrid_decode_step'

F32 = jnp.float32


def rmsnorm(x, g):
    xf = x.astype(F32)
    y = xf * lax.rsqrt(jnp.mean(xf * xf, axis=-1, keepdims=True) + NORM_EPS)
    return (y * g.astype(F32)).astype(x.dtype)


def partial_rope(x, pos):
    half = ROPE_DIM // 2
    inv = 1.0 / (ROPE_THETA ** (jnp.arange(0, ROPE_DIM, 2, dtype=F32) / ROPE_DIM))
    ang = pos.astype(F32)[:, None] * inv[None, :]
    cos = jnp.cos(ang)[None, :, None, :]
    sin = jnp.sin(ang)[None, :, None, :]
    xr = x[..., :ROPE_DIM].astype(F32)
    x1, x2 = xr[..., :half], xr[..., half:]
    rot = jnp.concatenate([x1 * cos - x2 * sin, x2 * cos + x1 * sin], axis=-1).astype(x.dtype)
    return jnp.concatenate([rot, x[..., ROPE_DIM:]], axis=-1)


def rwkv7_time_mix(z, s0, p):
    dt = z.dtype
    b, t, _ = z.shape
    z = z.astype(F32)
    w = RW_WIDTH
    o1 = 3 * w
    o2 = o1 + RW_DECAY_LORA
    o3 = o2 + RW_AAA_LORA
    r, k, v = z[..., :w], z[..., w:2 * w], z[..., 2 * w:o1]
    xw, xa, xg = z[..., o1:o2], z[..., o2:o3], z[..., o3:]
    w_log = -jax.nn.softplus(-(p['rw_w0'] + jnp.tanh(xw) @ p['rw_w2'])) - 0.5
    decay = jnp.exp(-jnp.exp(w_log))
    a = jax.nn.sigmoid(p['rw_a0'] + xa @ p['rw_a2'])
    g = jax.nn.sigmoid(xg) @ p['rw_g2']
    heads = lambda u: u.reshape(b, t, RW_HEADS, RW_HEAD)
    kk = heads(k * p['rw_k_k'])
    kk = kk / jnp.maximum(jnp.sqrt(jnp.sum(kk * kk, axis=-1, keepdims=True)), 1e-12)
    k = k * (1.0 + (a - 1.0) * p['rw_k_a'])
    r_h, k_h, v_h, w_h, a_h = heads(r), heads(k), heads(v), heads(decay), heads(a)

    def step(s, inp):
        r_t, w_t, k_t, v_t, kk_t, a_t = inp
        s_kk = jnp.einsum('bhij,bhj->bhi', s, kk_t)
        s = (s * w_t[:, :, None, :]
             - s_kk[..., :, None] * (kk_t * a_t)[..., None, :]
             + v_t[..., :, None] * k_t[..., None, :])
        return s, jnp.einsum('bhij,bhj->bhi', s, r_t)

    xs = tuple(jnp.moveaxis(u, 1, 0) for u in (r_h, w_h, k_h, v_h, kk, a_h))
    s_final, y = lax.scan(step, s0.astype(F32), xs)
    y = jnp.moveaxis(y, 0, 1)
    mu = jnp.mean(y, axis=-1, keepdims=True)
    var = jnp.mean(jnp.square(y - mu), axis=-1, keepdims=True)
    y = ((y - mu) * lax.rsqrt(var + RW_GN_EPS) * p['rw_ln_w'].reshape(RW_HEADS, RW_HEAD)
         + p['rw_ln_b'].reshape(RW_HEADS, RW_HEAD))
    y = y + jnp.sum(r_h * k_h * p['rw_r_k'], axis=-1, keepdims=True) * v_h
    out = y.reshape(b, t, w) * g
    return out.astype(dt), s_final.astype(s0.dtype)


def moba_select(q, k_means, qpos):
    nb = k_means.shape[1]
    cur = qpos // MB_BLOCK
    s = jnp.einsum('bthd,bnhd->bthn', q.astype(F32), k_means)
    past = jnp.arange(nb)[None, None, None, :] < cur[None, :, None, None]
    s = jnp.where(past, s, -jnp.inf)
    _, idx = lax.top_k(s, min(MB_TOPK, nb))
    valid = idx < cur[None, :, None, None]
    own = jnp.broadcast_to(cur[None, :, None, None], idx.shape[:3] + (1,)).astype(idx.dtype)
    blocks = jnp.concatenate([idx, own], axis=-1)
    valid = jnp.concatenate([valid, jnp.ones(own.shape, dtype=bool)], axis=-1)
    return blocks, valid


def moba_attend(q, qpos, kb, vb, blocks, valid):
    b, t, h, ns, j, d = kb.shape
    keypos = blocks[..., None] * MB_BLOCK + jnp.arange(MB_BLOCK)
    mask = valid[..., None] & (keypos <= qpos[None, :, None, None, None])
    logits = jnp.einsum('bthd,bthsjd->bthsj', q, kb, preferred_element_type=F32) * MB_SCALE
    logits = jnp.where(mask, logits, -jnp.inf).reshape(b, t, h, ns * j)
    probs = jax.nn.softmax(logits, axis=-1).reshape(b, t, h, ns, j)
    out = jnp.einsum('bthsj,bthsjd->bthd', probs, vb)
    return out.astype(q.dtype)


def moba_prompt(q, k, v):
    b, s, h, d = q.shape
    s_pad = -(-s // MB_BLOCK) * MB_BLOCK
    nb = s_pad // MB_BLOCK
    pad = ((0, 0), (0, s_pad - s), (0, 0), (0, 0))
    qp, kp, vp = jnp.pad(q, pad), jnp.pad(k, pad), jnp.pad(v, pad)
    kblk = kp.reshape(b, nb, MB_BLOCK, h, d)
    k_means = jnp.mean(kblk.astype(F32), axis=2)
    k_bh = kblk.transpose(0, 3, 1, 2, 4)
    v_bh = vp.reshape(b, nb, MB_BLOCK, h, d).transpose(0, 3, 1, 2, 4)
    n_chunks = s_pad // MB_Q_CHUNK
    q_chunks = qp.reshape(b, n_chunks, MB_Q_CHUNK, h, d).transpose(1, 0, 2, 3, 4)
    b_idx = jnp.arange(b)[:, None, None, None]
    h_idx = jnp.arange(h)[None, None, :, None]

    def one_chunk(args):
        q_c, c = args
        qpos = c * MB_Q_CHUNK + jnp.arange(MB_Q_CHUNK)
        blocks, valid = moba_select(q_c, k_means, qpos)
        kb = k_bh[b_idx, h_idx, blocks]
        vb = v_bh[b_idx, h_idx, blocks]
        return moba_attend(q_c, qpos, kb, vb, blocks, valid)

    out = lax.map(one_chunk, (q_chunks, jnp.arange(n_chunks)))
    return out.transpose(1, 0, 2, 3, 4).reshape(b, s_pad, h, d)[:, :s]


def moba_sample(q, k, v, cache_k, cache_v, page_table):
    db, t, h, d = q.shape
    n_pages = page_table.shape[1]
    past = n_pages * PAGE_SIZE
    ppb = MB_BLOCK // PAGE_SIZE
    n_new = -(-t // PAGE_SIZE)
    pad_new = ((0, 0), (0, n_new * PAGE_SIZE - t), (0, 0), (0, 0))
    k_new_pages = jnp.pad(k, pad_new).reshape(db, n_new, PAGE_SIZE, h, d)
    v_new_pages = jnp.pad(v, pad_new).reshape(db, n_new, PAGE_SIZE, h, d)
    page_sums = jnp.concatenate([
        jnp.sum(cache_k[page_table].astype(F32), axis=2),
        jnp.sum(k_new_pages.astype(F32), axis=2)], axis=1)
    n_lp = n_pages + n_new
    nb = -(-n_lp // ppb)
    page_sums = jnp.pad(page_sums, ((0, 0), (0, nb * ppb - n_lp), (0, 0), (0, 0)))
    k_means = jnp.sum(page_sums.reshape(db, nb, ppb, h, d), axis=2) / MB_BLOCK
    qpos = past + jnp.arange(t)
    blocks, valid = moba_select(q, k_means, qpos)
    ns = blocks.shape[-1]
    lp = blocks[..., None] * ppb + jnp.arange(ppb)
    b_idx = jnp.arange(db)[:, None, None, None, None]
    h_idx = jnp.arange(h)[None, None, :, None, None]
    in_cache = (lp < n_pages)[..., None, None]
    phys = page_table[b_idx, jnp.clip(lp, 0, n_pages - 1)]
    lp_new = jnp.clip(lp - n_pages, 0, n_new - 1)

    def gather_blocks(pool, new_pages):
        sel = jnp.where(in_cache, pool[phys, :, h_idx, :], new_pages[b_idx, lp_new, :, h_idx, :])
        return sel.reshape(db, t, h, ns, MB_BLOCK, d)

    kb = gather_blocks(cache_k, k_new_pages)
    vb = gather_blocks(cache_v, v_new_pages)
    return moba_attend(q, qpos, kb, vb, blocks, valid)


def memory_kv(mem, g, wk, wv):
    b, m, _ = mem.shape
    mn = rmsnorm(mem, g)
    return (mn @ wk).reshape(b, m, X_HEADS, X_HEAD), (mn @ wv).reshape(b, m, X_HEADS, X_HEAD)


def cross_attention(xn, mk, mv, wq, wo):
    b, t, _ = xn.shape
    q = (xn @ wq).reshape(b, t, X_HEADS, X_HEAD)
    logits = jnp.einsum('bthd,bmhd->bhtm', q, mk, preferred_element_type=F32) * (X_HEAD ** -0.5)
    probs = jax.nn.softmax(logits, axis=-1)
    o = jnp.einsum('bhtm,bmhd->bthd', probs, mv).reshape(b, t, X_WIDTH).astype(xn.dtype)
    return o @ wo


def conv_ffn(xn, prev, p):
    t = xn.shape[1]
    gt = xn @ p['w_gate']
    up = xn @ p['w_up']
    full = jnp.concatenate([prev.astype(gt.dtype), gt], axis=1)
    acc = p['conv_b'] + full[:, 0:t] * p['conv_w'][0]
    for j in range(1, CONV_W):
        acc = acc + full[:, j:j + t] * p['conv_w'][j]
    hdn = jax.nn.silu(acc) * up
    return hdn @ p['w_down'], full[:, t:]


def layer_forward(x, pos0, rw_state, rw_prev, conv_prev, mem_k, mem_v, moba_fn, p):
    b, t, _ = x.shape
    xn = rmsnorm(x, p['norm_mix'])
    proj = xn @ p['w_in']
    z_rw = proj[..., :RW_COLS]
    z_mb = proj[..., RW_COLS:RW_COLS + MB_COLS]
    z_g = proj[..., RW_COLS + MB_COLS:]
    prev_seq = jnp.concatenate([rw_prev[:, None, :].astype(z_rw.dtype), z_rw[:, :-1]], axis=1)
    z_rw_mixed = z_rw + p['rw_mu'] * (prev_seq - z_rw)
    o_a, rw_state_new = rwkv7_time_mix(z_rw_mixed, rw_state, p)
    pos = pos0 + jnp.arange(t)
    q = partial_rope(z_mb[..., :MB_WIDTH].reshape(b, t, MB_HEADS, MB_HEAD), pos)
    k = partial_rope(z_mb[..., MB_WIDTH:2 * MB_WIDTH].reshape(b, t, MB_HEADS, MB_HEAD), pos)
    v = z_mb[..., 2 * MB_WIDTH:].reshape(b, t, MB_HEADS, MB_HEAD)
    o_b = moba_fn(q, k, v).reshape(b, t, MB_WIDTH)
    g_a = jax.nn.sigmoid(z_g[..., :D_MODEL])
    g_b = jax.nn.sigmoid(z_g[..., D_MODEL:])
    merged = g_a * (o_a @ p['w_a_out']) + g_b * (o_b @ p['w_b_out'])
    x = x + merged @ p['w_o']
    x = x + cross_attention(rmsnorm(x, p['norm_x']), mem_k, mem_v, p['wx_q'], p['wx_o'])
    f, conv_new = conv_ffn(rmsnorm(x, p['norm_ffn']), conv_prev, p)
    x = x + f
    return x, k, v, rw_state_new, z_rw[:, -1], conv_new


def setup_inputs(seed: int = 0) -> dict:
    key = jax.random.key(seed)
    keys = list(jax.random.split(key, 48))

    def nrm(shape, scale):
        return jax.random.normal(keys.pop(), shape, F32) * scale

    def gain(n):
        return 1.0 + nrm((DEPTH, n), 0.02)

    L = DEPTH
    n_pages = PAST_LEN // PAGE_SIZE
    n_used = DEC_BATCH * n_pages
    n_pool = n_used + max(1, n_used // 4)
    page_table = jax.random.permutation(keys.pop(), n_pool)[:n_used].reshape(DEC_BATCH, n_pages).astype(jnp.int32)
    return {
        'x_prompt': nrm((BATCH, SEQ, D_MODEL), 1.0),
        'x_sample': nrm((DEC_BATCH, DEC_SEQ, D_MODEL), 1.0),
        'mem_prompt': nrm((BATCH, N_MEM, D_MODEL), 1.0),
        'cache_moba_k': nrm((L, n_pool, PAGE_SIZE, MB_HEADS, MB_HEAD), 1.0),
        'cache_moba_v': nrm((L, n_pool, PAGE_SIZE, MB_HEADS, MB_HEAD), 1.0),
        'page_table': page_table,
        'cache_mem_k': nrm((L, DEC_BATCH, N_MEM, X_HEADS, X_HEAD), 1.0),
        'cache_mem_v': nrm((L, DEC_BATCH, N_MEM, X_HEADS, X_HEAD), 1.0),
        'state_rwkv_wkv': nrm((L, DEC_BATCH, RW_HEADS, RW_HEAD, RW_HEAD), 0.3),
        'state_rwkv_shift': nrm((L, DEC_BATCH, RW_COLS), 1.0),
        'state_ffn_conv': nrm((L, DEC_BATCH, CONV_W - 1, D_FF), 1.0),
        'norm_mix': gain(D_MODEL),
        'w_in': nrm((L, D_MODEL, N_IN), D_MODEL ** -0.5),
        'rw_mu': jax.random.uniform(keys.pop(), (L, RW_COLS), F32),
        'rw_w0': nrm((L, RW_WIDTH), 0.5),
        'rw_w2': nrm((L, RW_DECAY_LORA, RW_WIDTH), RW_DECAY_LORA ** -0.5),
        'rw_a0': nrm((L, RW_WIDTH), 0.5),
        'rw_a2': nrm((L, RW_AAA_LORA, RW_WIDTH), RW_AAA_LORA ** -0.5),
        'rw_g2': nrm((L, RW_GATE_LORA, RW_WIDTH), RW_GATE_LORA ** -0.5),
        'rw_k_k': 0.85 + nrm((L, RW_WIDTH), 0.05),
        'rw_k_a': 1.0 + nrm((L, RW_WIDTH), 0.05),
        'rw_r_k': nrm((L, RW_HEADS, RW_HEAD), 0.1),
        'rw_ln_w': gain(RW_WIDTH),
        'rw_ln_b': nrm((L, RW_WIDTH), 0.01),
        'w_a_out': nrm((L, RW_WIDTH, D_MODEL), RW_WIDTH ** -0.5),
        'w_b_out': nrm((L, MB_WIDTH, D_MODEL), MB_WIDTH ** -0.5),
        'w_o': nrm((L, D_MODEL, D_MODEL), D_MODEL ** -0.5),
        'norm_x': gain(D_MODEL),
        'norm_mem': gain(D_MODEL),
        'wx_q': nrm((L, D_MODEL, X_WIDTH), D_MODEL ** -0.5),
        'wx_k': nrm((L, D_MODEL, X_WIDTH), D_MODEL ** -0.5),
        'wx_v': nrm((L, D_MODEL, X_WIDTH), D_MODEL ** -0.5),
        'wx_o': nrm((L, X_WIDTH, D_MODEL), X_WIDTH ** -0.5),
        'norm_ffn': gain(D_MODEL),
        'w_gate': nrm((L, D_MODEL, D_FF), D_MODEL ** -0.5),
        'w_up': nrm((L, D_MODEL, D_FF), D_MODEL ** -0.5),
        'conv_w': nrm((L, CONV_W, D_FF), CONV_W ** -0.5),
        'conv_b': nrm((L, D_FF), 0.01),
        'w_down': nrm((L, D_FF, D_MODEL), D_FF ** -0.5),
        'norm_out': 1.0 + nrm((D_MODEL,), 0.02),
    }


def reference(x_prompt, x_sample, mem_prompt, cache_moba_k, cache_moba_v, page_table,
              cache_mem_k, cache_mem_v, state_rwkv_wkv, state_rwkv_shift, state_ffn_conv,
              norm_mix, w_in, rw_mu, rw_w0, rw_w2, rw_a0, rw_a2, rw_g2, rw_k_k, rw_k_a, rw_r_k,
              rw_ln_w, rw_ln_b, w_a_out, w_b_out, w_o, norm_x, norm_mem, wx_q, wx_k, wx_v, wx_o,
              norm_ffn, w_gate, w_up, conv_w, conv_b, w_down, norm_out):
    past_len = page_table.shape[1] * PAGE_SIZE
    xp, xs = x_prompt, x_sample
    bp = xp.shape[0]
    kp_l, vp_l, ks_l, vs_l, mkp_l, mvp_l = [], [], [], [], [], []
    swp_l, sws_l, shp_l, shs_l, cvp_l, cvs_l = [], [], [], [], [], []
    for l in range(DEPTH):
        p = {
            'norm_mix': norm_mix[l], 'w_in': w_in[l], 'rw_mu': rw_mu[l], 'rw_w0': rw_w0[l],
            'rw_w2': rw_w2[l], 'rw_a0': rw_a0[l], 'rw_a2': rw_a2[l], 'rw_g2': rw_g2[l],
            'rw_k_k': rw_k_k[l], 'rw_k_a': rw_k_a[l], 'rw_r_k': rw_r_k[l], 'rw_ln_w': rw_ln_w[l],
            'rw_ln_b': rw_ln_b[l], 'w_a_out': w_a_out[l], 'w_b_out': w_b_out[l], 'w_o': w_o[l],
            'norm_x': norm_x[l], 'wx_q': wx_q[l], 'wx_o': wx_o[l], 'norm_ffn': norm_ffn[l],
            'w_gate': w_gate[l], 'w_up': w_up[l], 'conv_w': conv_w[l], 'conv_b': conv_b[l],
            'w_down': w_down[l],
        }
        mk_p, mv_p = memory_kv(mem_prompt, norm_mem[l], wx_k[l], wx_v[l])
        xp, k_p, v_p, sw_p, sh_p, cv_p = layer_forward(
            xp, 0,
            jnp.zeros((bp, RW_HEADS, RW_HEAD, RW_HEAD), xp.dtype),
            jnp.zeros((bp, RW_COLS), xp.dtype),
            jnp.zeros((bp, CONV_W - 1, D_FF), xp.dtype),
            mk_p, mv_p, moba_prompt, p)
        moba_s = functools.partial(moba_sample, cache_k=cache_moba_k[l], cache_v=cache_moba_v[l],
                                   page_table=page_table)
        xs, k_s, v_s, sw_s, sh_s, cv_s = layer_forward(
            xs, past_len, state_rwkv_wkv[l], state_rwkv_shift[l], state_ffn_conv[l],
            cache_mem_k[l], cache_mem_v[l], moba_s, p)
        kp_l.append(k_p); vp_l.append(v_p); ks_l.append(k_s); vs_l.append(v_s)
        mkp_l.append(mk_p); mvp_l.append(mv_p)
        swp_l.append(sw_p); sws_l.append(sw_s); shp_l.append(sh_p); shs_l.append(sh_s)
        cvp_l.append(cv_p); cvs_l.append(cv_s)
    y_prompt = rmsnorm(xp, norm_out)
    y_sample = rmsnorm(xs, norm_out)
    return (y_prompt, y_sample,
            jnp.stack(kp_l), jnp.stack(vp_l), jnp.stack(ks_l), jnp.stack(vs_l),
            jnp.stack(mkp_l), jnp.stack(mvp_l),
            jnp.stack(swp_l), jnp.stack(sws_l), jnp.stack(shp_l), jnp.stack(shs_l),
            jnp.stack(cvp_l), jnp.stack(cvs_l))
```

```python
import functools
import math

import jax
import jax.numpy as jnp
from jax import lax
from jax.experimental import pallas as pl
from jax.experimental.pallas import tpu as pltpu

F32 = jnp.float32
BF16 = jnp.bfloat16

NORM_EPS = 1e-5
RW_HEAD = 64
RW_GN_EPS = 64e-5
MB_HEAD = 128
MB_BLOCK = 256
MB_TOPK = 3
PAGE_SIZE = 128
ROPE_DIM = MB_HEAD // 4
ROPE_THETA = 500000.0
X_HEAD = 128
CONV_W = 3

LANE = 128
VMEM_LIMIT = 56 * 1024 * 1024
SCAN_CHUNK = 64
SCAN_HEADS = 8
STEP_CHUNK = 8


def _params(sem):
    return pltpu.CompilerParams(dimension_semantics=sem, vmem_limit_bytes=VMEM_LIMIT)


def _pick(n, prefs):
    for p in prefs:
        if n % p == 0:
            return p
    return n


def _dot(a, b, dims=(((1,), (0,)), ((), ())), exact=False):
    if exact:
        return lax.dot_general(a, b, dims, precision=lax.Precision.HIGHEST,
                               preferred_element_type=F32)
    return lax.dot_general(a.astype(BF16), b.astype(BF16), dims, preferred_element_type=F32)


_NT = (((1,), (1,)), ((), ()))
_TN = (((0,), (0,)), ((), ()))


def _sigmoid(x):
    return 1.0 / (1.0 + jnp.exp(-x))


def _rmsnorm_kernel(x_ref, g_ref, o_ref):
    x = x_ref[...]
    y = x * lax.rsqrt(jnp.mean(x * x, axis=-1, keepdims=True) + NORM_EPS)
    o_ref[...] = (y * g_ref[...]).astype(o_ref.dtype)


def _rmsnorm(x, g, out_dtype):
    m, d = x.shape
    tr = _pick(m, (256, 128, 64, 32, 16, 8))
    return pl.pallas_call(
        _rmsnorm_kernel,
        grid=(m // tr,),
        in_specs=[pl.BlockSpec((tr, d), lambda i: (i, 0)),
                  pl.BlockSpec((1, d), lambda i: (0, 0))],
        out_specs=pl.BlockSpec((tr, d), lambda i: (i, 0)),
        out_shape=jax.ShapeDtypeStruct((m, d), out_dtype),
        compiler_params=_params(("parallel",)),
        name="rmsnorm",
    )(x, g.reshape(1, d))


def _mm_kernel(*refs, nk, has_res):
    if has_res:
        a_ref, w_ref, r_ref, o_ref = refs[:4]
        scratch = refs[4:]
    else:
        a_ref, w_ref, o_ref = refs[:3]
        r_ref = None
        scratch = refs[3:]
    part = _dot(a_ref[...], w_ref[...])
    if nk == 1:
        if has_res:
            part = part + r_ref[...]
        o_ref[...] = part.astype(o_ref.dtype)
        return
    acc_ref = scratch[0]
    k = pl.program_id(2)

    @pl.when(k == 0)
    def _():
        acc_ref[...] = part

    @pl.when(k > 0)
    def _():
        acc_ref[...] += part

    @pl.when(k == nk - 1)
    def _():
        r = acc_ref[...]
        if has_res:
            r = r + r_ref[...]
        o_ref[...] = r.astype(o_ref.dtype)


def _matmul(a, w, res=None, out_dtype=F32, tm=None, tn=256, nk=1):
    m, kdim = a.shape
    n = w.shape[1]
    tm = tm or _pick(m, (1024, 512, 256, 128, 64, 32, 16, 8))
    tn = min(tn, n)
    tk = kdim // nk
    assert m % tm == 0 and kdim % nk == 0 and (nk == 1 or tk % LANE == 0)
    in_specs = [pl.BlockSpec((tm, tk), lambda i, j, k: (i, k)),
                pl.BlockSpec((tk, tn), lambda i, j, k: (k, j))]
    args = [a, w]
    if res is not None:
        in_specs.append(pl.BlockSpec((tm, tn), lambda i, j, k: (i, j)))
        args.append(res)
    scratch = [pltpu.VMEM((tm, tn), F32)] if nk > 1 else []
    return pl.pallas_call(
        functools.partial(_mm_kernel, nk=nk, has_res=res is not None),
        grid=(m // tm, pl.cdiv(n, tn), nk),
        in_specs=in_specs,
        out_specs=pl.BlockSpec((tm, tn), lambda i, j, k: (i, j)),
        out_shape=jax.ShapeDtypeStruct((m, n), out_dtype),
        scratch_shapes=scratch,
        compiler_params=_params(("parallel", "parallel", "arbitrary")),
        name="matmul",
    )(*args)


def _prep_kernel(zr_ref, zk_ref, zv_ref, zl_ref, pr_ref, pk_ref, pv_ref, pl_ref,
                 mur_ref, muk_ref, muv_ref, mul_ref, w0_ref, a0_ref, kk_ref, ka_ref,
                 w2_ref, a2_ref, g2_ref,
                 r_ref, lw_ref, k_ref, v_ref, kkr_ref, a_ref, g_ref,
                 cr, ck, cv, cl, *, tp):
    t = pl.program_id(1)

    @pl.when(t == 0)
    def _():
        cr[...] = pr_ref[...]
        ck[...] = pk_ref[...]
        cv[...] = pv_ref[...]
        cl[...] = pl_ref[...]

    def shift_mix(z_ref, carry, mu_ref):
        z = z_ref[...]
        if tp == 1:
            prev = carry[...]
        else:
            first = lax.broadcasted_iota(jnp.int32, z.shape, 0) == 0
            prev = jnp.where(first, carry[...], pltpu.roll(z, 1, 0))
        carry[...] = z[tp - 1:tp, :]
        return z + mu_ref[...] * (prev - z)

    r = shift_mix(zr_ref, cr, mur_ref)
    k = shift_mix(zk_ref, ck, muk_ref)
    v = shift_mix(zv_ref, cv, muv_ref)
    xl = shift_mix(zl_ref, cl, mul_ref)
    u = w0_ref[...] + _dot(jnp.tanh(xl), w2_ref[...])
    lw = -math.exp(-0.5) * _sigmoid(u)
    a = _sigmoid(a0_ref[...] + _dot(xl, a2_ref[...]))
    g = _dot(_sigmoid(xl), g2_ref[...])
    r_ref[...] = r
    lw_ref[...] = lw
    k_ref[...] = k * (1.0 + (a - 1.0) * ka_ref[...])
    v_ref[...] = v
    kkr_ref[...] = k * kk_ref[...]
    a_ref[...] = a
    g_ref[...] = g


def _rwkv_prep(proj, prev, lay, prm):
    b, t, _ = proj.shape
    rw, lp = lay["rw"], lay["lp"]
    tp = _pick(t, (128, 64, 32, 16, 8))
    assert lay["off_l"] % lp == 0
    lblk = lay["off_l"] // lp
    row = lambda j: pl.BlockSpec((None, tp, rw), lambda bi, ti, j=j: (bi, ti, j))
    prev_spec = lambda w: pl.BlockSpec((None, 1, w), lambda bi, ti: (bi, 0, 0))
    vec = lambda w: pl.BlockSpec((1, w), lambda bi, ti: (0, 0))
    mat = pl.BlockSpec((lp, rw), lambda bi, ti: (0, 0))
    out_spec = pl.BlockSpec((None, tp, rw), lambda bi, ti: (bi, ti, 0))
    out_sds = jax.ShapeDtypeStruct((b, t, rw), F32)
    return pl.pallas_call(
        functools.partial(_prep_kernel, tp=tp),
        grid=(b, t // tp),
        in_specs=[row(0), row(1), row(2),
                  pl.BlockSpec((None, tp, lp), lambda bi, ti: (bi, ti, lblk)),
                  prev_spec(rw), prev_spec(rw), prev_spec(rw), prev_spec(lp),
                  vec(rw), vec(rw), vec(rw), vec(lp), vec(rw), vec(rw), vec(rw), vec(rw),
                  mat, mat, mat],
        out_specs=[out_spec] * 7,
        out_shape=[out_sds] * 7,
        scratch_shapes=[pltpu.VMEM((1, rw), F32), pltpu.VMEM((1, rw), F32),
                        pltpu.VMEM((1, rw), F32), pltpu.VMEM((1, lp), F32)],
        compiler_params=_params(("parallel", "arbitrary")),
        name="rwkv_prep",
    )(proj, proj, proj, proj, *prev,
      prm["mu_r"], prm["mu_k"], prm["mu_v"], prm["mu_l"], prm["w0"], prm["a0"],
      prm["k_k"], prm["k_a"], prm["w2p"], prm["a2p"], prm["g2p"])


def _scan_kernel(r_ref, lw_ref, k_ref, v_ref, kkr_ref, a_ref, g_ref, rk_ref, lnw_ref, lnb_ref,
                 s0_ref, o_ref, sf_ref, s_scr, *, chunk, heads, n_chunks):
    c = pl.program_id(2)

    @pl.when(c == 0)
    def _():
        s_scr[...] = s0_ref[...]

    n = chunk
    row = lax.broadcasted_iota(jnp.int32, (n, n), 0)
    col = lax.broadcasted_iota(jnp.int32, (n, n), 1)
    incl = row >= col
    strict = row > col
    eye = (row == col).astype(F32)

    lw = lw_ref[...]
    cum = _dot(incl.astype(F32), lw, exact=True)
    cum_prev = cum - lw
    total = cum[n - 1:n, :]
    e_cum = jnp.exp(cum)
    e_prev = jnp.exp(cum_prev)
    e_neg = jnp.exp(-cum)
    e_rest = jnp.exp(total - cum)
    e_total = jnp.exp(total)

    r_all, k_all, v_all = r_ref[...], k_ref[...], v_ref[...]
    kkr_all, a_all, g_all = kkr_ref[...], a_ref[...], g_ref[...]
    rk_all, lnw_all, lnb_all = rk_ref[...], lnw_ref[...], lnb_ref[...]

    outs = []
    for h in range(heads):
        sl = slice(h * RW_HEAD, (h + 1) * RW_HEAD)
        r, k, v = r_all[:, sl], k_all[:, sl], v_all[:, sl]
        kkr = kkr_all[:, sl]
        kk = kkr / jnp.maximum(jnp.sqrt(jnp.sum(kkr * kkr, axis=-1, keepdims=True)), 1e-12)
        bb = kk * a_all[:, sl]
        q2 = jnp.concatenate([kk * e_prev[:, sl], r * e_cum[:, sl]], axis=0)
        p_k = _dot(q2, k * e_neg[:, sl], _NT)
        p_b = _dot(q2, bb * e_neg[:, sl], _NT)
        m_k = jnp.where(strict, p_k[:n], 0.0)
        m_b = jnp.where(strict, p_b[:n], 0.0)
        a_k = jnp.where(incl, p_k[n:], 0.0)
        a_b = jnp.where(incl, p_b[n:], 0.0)
        npow = -m_b
        tinv = eye + npow
        for _ in range(int(math.log2(n)) - 1):
            npow = _dot(npow, npow, exact=True)
            tinv = tinv + _dot(tinv, npow, exact=True)
        s = s_scr[h]
        qs = _dot(q2, s, _NT)
        u = _dot(tinv, qs[:n] + _dot(m_k, v))
        y = qs[n:] + _dot(a_k, v) - _dot(a_b, u)
        s_scr[h] = (s * e_total[:, sl] + _dot(v, k * e_rest[:, sl], _TN)
                    - _dot(u, bb * e_rest[:, sl], _TN))
        mu = jnp.mean(y, axis=-1, keepdims=True)
        yc = y - mu
        var = jnp.mean(yc * yc, axis=-1, keepdims=True)
        yn = yc * lax.rsqrt(var + RW_GN_EPS) * lnw_all[:, sl] + lnb_all[:, sl]
        bonus = jnp.sum(r * k * rk_all[:, sl], axis=-1, keepdims=True) * v
        outs.append((yn + bonus) * g_all[:, sl])
    o_ref[...] = jnp.concatenate(outs, axis=1).astype(o_ref.dtype)

    @pl.when(c == n_chunks - 1)
    def _():
        sf_ref[...] = s_scr[...]


def _rwkv_scan(streams, s0, prm, chunk):
    b, t, rw = streams[0].shape
    nh = rw // RW_HEAD
    hg = _pick(nh, (SCAN_HEADS, 4, 2))
    w = hg * RW_HEAD
    assert t % chunk == 0 and chunk & (chunk - 1) == 0 and w % LANE == 0
    n_chunks = t // chunk
    stream_spec = pl.BlockSpec((None, chunk, w), lambda bi, gi, ci: (bi, ci, gi))
    vec_spec = pl.BlockSpec((1, w), lambda bi, gi, ci: (0, gi))
    state_spec = pl.BlockSpec((None, hg, RW_HEAD, RW_HEAD), lambda bi, gi, ci: (bi, gi, 0, 0))
    return pl.pallas_call(
        functools.partial(_scan_kernel, chunk=chunk, heads=hg, n_chunks=n_chunks),
        grid=(b, nh // hg, n_chunks),
        in_specs=[stream_spec] * 7 + [vec_spec] * 3 + [state_spec],
        out_specs=[stream_spec, state_spec],
        out_shape=[jax.ShapeDtypeStruct((b, t, rw), BF16),
                   jax.ShapeDtypeStruct((b, nh, RW_HEAD, RW_HEAD), F32)],
        scratch_shapes=[pltpu.VMEM((hg, RW_HEAD, RW_HEAD), F32)],
        compiler_params=_params(("parallel", "parallel", "arbitrary")),
        name="rwkv_scan",
    )(*streams, prm["r_k"], prm["ln_w"], prm["ln_b"], s0)


def _rope_tables(pos):
    half = ROPE_DIM // 2
    inv = 1.0 / (ROPE_THETA ** (jnp.arange(0, ROPE_DIM, 2, dtype=F32) / ROPE_DIM))
    ang = pos.astype(F32)[:, None] * inv[None, :]
    cos, sin = jnp.cos(ang), jnp.sin(ang)
    t = pos.shape[0]
    rest = MB_HEAD - ROPE_DIM
    c = jnp.concatenate([cos, cos, jnp.ones((t, rest), F32)], axis=1)
    s = jnp.concatenate([-sin, sin, jnp.zeros((t, rest), F32)], axis=1)
    del half
    return c, s


def _rope(x, c, s):
    half = ROPE_DIM // 2
    lane = lax.broadcasted_iota(jnp.int32, x.shape, 1)
    swapped = jnp.where(lane < half, pltpu.roll(x, MB_HEAD - half, 1), pltpu.roll(x, half, 1))
    return x * c + swapped * s


def _moba_prompt_kernel(q_ref, k_ref, v_ref, c_ref, s_ref, kout_ref, o_ref, *, nb):
    c, s = c_ref[...], s_ref[...]
    q = _rope(q_ref[...], c, s)
    k = _rope(k_ref[...], c, s)
    kout_ref[...] = k
    t = q.shape[0]
    k_means = jnp.sum(k.reshape(nb, MB_BLOCK, MB_HEAD), axis=1) * (1.0 / MB_BLOCK)
    scores = _dot(q, k_means, _NT, exact=True)
    qb = q.astype(BF16)
    kb = k.astype(BF16)
    vb = v_ref[...].astype(BF16)
    scale = MB_HEAD ** -0.5
    rowi = lax.broadcasted_iota(jnp.int32, (MB_BLOCK, MB_BLOCK), 0)
    coli = lax.broadcasted_iota(jnp.int32, (MB_BLOCK, MB_BLOCK), 1)
    causal = rowi >= coli
    neg_inf = float("-inf")
    for i in range(nb):
        rows = slice(i * MB_BLOCK, (i + 1) * MB_BLOCK)
        q_i = qb[rows]
        sc = scores[rows]
        logits = []
        for j in range(i + 1):
            cols = slice(j * MB_BLOCK, (j + 1) * MB_BLOCK)
            lg = _dot(q_i, kb[cols], _NT) * scale
            if j == i:
                mask = causal
            elif i <= MB_TOPK:
                mask = None
            else:
                sj = sc[:, j:j + 1]
                rank = jnp.zeros((MB_BLOCK, 1), jnp.int32)
                for m in range(i):
                    if m == j:
                        continue
                    sm = sc[:, m:m + 1]
                    ahead = (sm > sj) | (sm == sj) if m < j else (sm > sj)
                    rank = rank + ahead.astype(jnp.int32)
                mask = rank < MB_TOPK
            if mask is not None:
                lg = jnp.where(mask, lg, neg_inf)
            logits.append(lg)
        mx = logits[0].max(axis=-1, keepdims=True)
        for lg in logits[1:]:
            mx = jnp.maximum(mx, lg.max(axis=-1, keepdims=True))
        den = jnp.zeros((MB_BLOCK, 1), F32)
        acc = jnp.zeros((MB_BLOCK, MB_HEAD), F32)
        for j, lg in enumerate(logits):
            pj = jnp.exp(lg - mx)
            den = den + jnp.sum(pj, axis=-1, keepdims=True)
            acc = acc + _dot(pj, vb[j * MB_BLOCK:(j + 1) * MB_BLOCK])
        o_ref[rows, :] = (acc / den).astype(o_ref.dtype)
    del t


def _moba_prompt(proj, lay):
    b, t, _ = proj.shape
    assert t % MB_BLOCK == 0
    mbw = lay["mb"]
    nh = mbw // MB_HEAD
    ctab, stab = _rope_tables(jnp.arange(t))
    qb, kb, vb = lay["off_q"] // MB_HEAD, lay["off_mk"] // MB_HEAD, lay["off_mv"] // MB_HEAD
    head = lambda off: pl.BlockSpec((None, t, MB_HEAD), lambda bi, hi, off=off: (bi, 0, off + hi))
    tab = pl.BlockSpec((t, MB_HEAD), lambda bi, hi: (0, 0))
    out = pl.BlockSpec((None, t, MB_HEAD), lambda bi, hi: (bi, 0, hi))
    return pl.pallas_call(
        functools.partial(_moba_prompt_kernel, nb=t // MB_BLOCK),
        grid=(b, nh),
        in_specs=[head(qb), head(kb), head(vb), tab, tab],
        out_specs=[out, out],
        out_shape=[jax.ShapeDtypeStruct((b, t, mbw), F32), jax.ShapeDtypeStruct((b, t, mbw), BF16)],
        compiler_params=_params(("parallel", "parallel")),
        name="moba_prompt",
    )(proj, proj, proj, ctab, stab)


def _rope_rows_kernel(q_ref, k_ref, c_ref, s_ref, qo_ref, ko_ref, *, heads):
    c, s = c_ref[...], s_ref[...]
    for h in range(heads):
        sl = slice(h * MB_HEAD, (h + 1) * MB_HEAD)
        qo_ref[:, sl] = _rope(q_ref[:, sl], c, s)
        ko_ref[:, sl] = _rope(k_ref[:, sl], c, s)


def _rope_rows(proj2d, lay, pos):
    m = proj2d.shape[0]
    mbw = lay["mb"]
    ctab, stab = _rope_tables(jnp.full((1,), pos))
    assert lay["off_q"] % mbw == 0 and lay["off_mk"] % mbw == 0
    spec = lambda off: pl.BlockSpec((m, mbw), lambda i, off=off: (0, off // mbw))
    tab = pl.BlockSpec((1, MB_HEAD), lambda i: (0, 0))
    out = pl.BlockSpec((m, mbw), lambda i: (0, 0))
    sds = jax.ShapeDtypeStruct((m, mbw), F32)
    return pl.pallas_call(
        functools.partial(_rope_rows_kernel, heads=mbw // MB_HEAD),
        grid=(1,),
        in_specs=[spec(lay["off_q"]), spec(lay["off_mk"]), tab, tab],
        out_specs=[out, out],
        out_shape=[sds, sds],
        compiler_params=_params(("arbitrary",)),
        name="rope_rows",
    )(proj2d, proj2d, ctab, stab)


def _block_sum_kernel(pt_ref, page_ref, o_ref):
    p = pl.program_id(2)
    part = jnp.sum(page_ref[...], axis=0)

    @pl.when(p == 0)
    def _():
        o_ref[...] = part

    @pl.when(p > 0)
    def _():
        o_ref[...] += part


def _block_sums(cache_k, page_table):
    _, page, nh, d = cache_k.shape
    db, n_pages = page_table.shape
    ppb = MB_BLOCK // page
    nb = n_pages // ppb
    return pl.pallas_call(
        _block_sum_kernel,
        grid_spec=pltpu.PrefetchScalarGridSpec(
            num_scalar_prefetch=1,
            grid=(db, nb, ppb),
            in_specs=[pl.BlockSpec((None, page, nh, d),
                                   lambda bi, ni, pi, pt: (pt[bi, ni * ppb + pi], 0, 0, 0))],
            out_specs=pl.BlockSpec((None, None, nh, d), lambda bi, ni, pi, pt: (bi, ni, 0, 0)),
        ),
        out_shape=jax.ShapeDtypeStruct((db, nb, nh, d), F32),
        compiler_params=_params(("parallel", "parallel", "arbitrary")),
        name="moba_block_sums",
    )(page_table, cache_k)


def _decode_select_kernel(q_ref, ks_ref, idx_ref, *, nb):
    q = q_ref[...]
    scores = jnp.sum(ks_ref[...] * (1.0 / MB_BLOCK) * q[None], axis=-1, keepdims=True)
    blk = lax.broadcasted_iota(jnp.int32, scores.shape, 0)
    rank = jnp.zeros(scores.shape, jnp.int32)
    for m in range(nb):
        sm = scores[m:m + 1]
        ahead = (sm > scores) | ((sm == scores) & (blk > m))
        rank = rank + ahead.astype(jnp.int32)
    for r in range(MB_TOPK):
        idx_ref[r] = jnp.sum(jnp.where(rank == r, blk, 0), axis=0)


def _decode_select(q, block_sums):
    db, nh, d = q.shape
    nb = block_sums.shape[1]
    assert nb >= MB_TOPK
    return pl.pallas_call(
        functools.partial(_decode_select_kernel, nb=nb),
        grid=(db,),
        in_specs=[pl.BlockSpec((None, nh, d), lambda bi: (bi, 0, 0)),
                  pl.BlockSpec((None, nb, nh, d), lambda bi: (bi, 0, 0, 0))],
        out_specs=pl.BlockSpec((None, MB_TOPK, nh, 1), lambda bi: (bi, 0, 0, 0)),
        out_shape=jax.ShapeDtypeStruct((db, MB_TOPK, nh, 1), jnp.int32),
        compiler_params=_params(("parallel",)),
        name="moba_decode_select",
    )(q, block_sums)


def _decode_attend_kernel(sel_ref, pt_ref, q_ref, kn_ref, vn_ref, *refs, n_sel):
    k_refs, v_refs, o_ref = refs[:n_sel], refs[n_sel:2 * n_sel], refs[2 * n_sel]
    scale = MB_HEAD ** -0.5
    q = q_ref[...]
    q8 = jnp.broadcast_to(q, (8, MB_HEAD))
    logits = [_dot(q8, kr[...], _NT)[0:1] * scale for kr in k_refs]
    own = jnp.sum(q * kn_ref[...], axis=-1, keepdims=True) * scale
    mx = own
    for lg in logits:
        mx = jnp.maximum(mx, lg.max(axis=-1, keepdims=True))
    p_own = jnp.exp(own - mx)
    den = p_own
    acc = p_own * vn_ref[...]
    for lg, vr in zip(logits, v_refs):
        pj = jnp.exp(lg - mx)
        den = den + jnp.sum(pj, axis=-1, keepdims=True)
        acc = acc + _dot(jnp.broadcast_to(pj, (8, pj.shape[1])), vr[...])[0:1]
    o_ref[...] = acc / den


def _decode_attend(q, k_new, v_new, sel, page_table, cache_k, cache_v):
    db, nh, d = q.shape
    page = cache_k.shape[1]
    ppb = MB_BLOCK // page
    n_sel = MB_TOPK * ppb
    vec = pl.BlockSpec((None, None, 1, d), lambda bi, hi, sel_r, pt_r: (bi, hi, 0, 0))

    def page_spec(r, j):
        return pl.BlockSpec(
            (None, page, d),
            lambda bi, hi, sel_r, pt_r: (
                pt_r[bi, sel_r[(bi * nh + hi) * MB_TOPK + r] * ppb + j], 0, hi))

    page_specs = [page_spec(r, j) for r in range(MB_TOPK) for j in range(ppb)]
    as4 = lambda x: x.reshape(db, nh, 1, d)
    out = pl.pallas_call(
        functools.partial(_decode_attend_kernel, n_sel=n_sel),
        grid_spec=pltpu.PrefetchScalarGridSpec(
            num_scalar_prefetch=2,
            grid=(db, nh),
            in_specs=[vec, vec, vec] + page_specs + page_specs,
            out_specs=vec,
        ),
        out_shape=jax.ShapeDtypeStruct((db, nh, 1, d), F32),
        compiler_params=_params(("parallel", "parallel")),
        name="moba_decode_attend",
    )(sel.reshape(-1), page_table, as4(q), as4(k_new), as4(v_new),
      *([cache_k] * n_sel), *([cache_v] * n_sel))
    return out.reshape(db, nh * d)


def _merge_kernel(oa_ref, ob_ref, wa_ref, wb_ref, ga_ref, gb_ref, o_ref):
    ya = _dot(oa_ref[...], wa_ref[...])
    yb = _dot(ob_ref[...], wb_ref[...])
    o_ref[...] = (_sigmoid(ga_ref[...]) * ya + _sigmoid(gb_ref[...]) * yb).astype(o_ref.dtype)


def _merge(o_a, o_b, w_a, w_b, proj2d, lay):
    m, ka = o_a.shape
    kb = o_b.shape[1]
    d = w_a.shape[1]
    tm = _pick(m, (1024, 512, 256, 128, 64, 32, 16, 8))
    tn = _pick(d, (256, 128))
    assert lay["off_ga"] % tn == 0 and lay["off_gb"] % tn == 0
    ga, gb = lay["off_ga"] // tn, lay["off_gb"] // tn
    return pl.pallas_call(
        _merge_kernel,
        grid=(m // tm, d // tn),
        in_specs=[pl.BlockSpec((tm, ka), lambda i, j: (i, 0)),
                  pl.BlockSpec((tm, kb), lambda i, j: (i, 0)),
                  pl.BlockSpec((ka, tn), lambda i, j: (0, j)),
                  pl.BlockSpec((kb, tn), lambda i, j: (0, j)),
                  pl.BlockSpec((tm, tn), lambda i, j: (i, ga + j)),
                  pl.BlockSpec((tm, tn), lambda i, j: (i, gb + j))],
        out_specs=pl.BlockSpec((tm, tn), lambda i, j: (i, j)),
        out_shape=jax.ShapeDtypeStruct((m, d), BF16),
        compiler_params=_params(("parallel", "parallel")),
        name="gated_merge",
    )(o_a, o_b, w_a, w_b, proj2d, proj2d)


def _xattn_kernel(q_ref, k_ref, v_ref, o_ref, *, heads):
    scale = X_HEAD ** -0.5
    for h in range(heads):
        sl = slice(h * X_HEAD, (h + 1) * X_HEAD)
        q = q_ref[:, sl]
        if q.shape[0] < 8:
            q = jnp.broadcast_to(q, (8, X_HEAD))
        lg = _dot(q, k_ref[:, sl], _NT) * scale
        p = jnp.exp(lg - lg.max(axis=-1, keepdims=True))
        o = _dot(p, v_ref[:, sl]) / jnp.sum(p, axis=-1, keepdims=True)
        o_ref[:, sl] = o[:o_ref.shape[0]].astype(o_ref.dtype)


def _xattn(q, mk, mv):
    b, t, xw = q.shape
    n_mem = mk.shape[1]
    tq = _pick(t, (512, 256, 128, 64, 32, 16, 8))
    qspec = pl.BlockSpec((None, tq, xw), lambda bi, ti: (bi, ti, 0))
    mspec = pl.BlockSpec((None, n_mem, xw), lambda bi, ti: (bi, 0, 0))
    return pl.pallas_call(
        functools.partial(_xattn_kernel, heads=xw // X_HEAD),
        grid=(b, t // tq),
        in_specs=[qspec, mspec, mspec],
        out_specs=qspec,
        out_shape=jax.ShapeDtypeStruct((b, t, xw), BF16),
        compiler_params=_params(("parallel", "parallel")),
        name="cross_attention",
    )(q, mk, mv)


def _ffn_in_seq_kernel(x_ref, wg_ref, wu_ref, cw_ref, cb_ref, prev_ref, h_ref, cn_ref, carry,
                       *, tm, n_t):
    t = pl.program_id(2)

    @pl.when(t == 0)
    def _():
        carry[...] = prev_ref[...]

    x = x_ref[...]
    gt = _dot(x, wg_ref[...])
    up = _dot(x, wu_ref[...])
    rowi = lax.broadcasted_iota(jnp.int32, gt.shape, 0)
    c0, c1 = carry[0:1, :], carry[1:2, :]
    g1 = jnp.where(rowi == 0, c1, pltpu.roll(gt, 1, 0))
    g2 = jnp.where(rowi == 0, c0, jnp.where(rowi == 1, c1, pltpu.roll(gt, 2, 0)))
    acc = cb_ref[...] + g2 * cw_ref[0:1, :] + g1 * cw_ref[1:2, :] + gt * cw_ref[2:3, :]
    h_ref[...] = (acc * _sigmoid(acc) * up).astype(h_ref.dtype)
    carry[...] = gt[tm - 2:tm, :]

    @pl.when(t == n_t - 1)
    def _():
        cn_ref[...] = gt[tm - 2:tm, :]


def _ffn_in_seq(xn, w_gate, w_up, conv_w, conv_b, conv_prev):
    b, t, d = xn.shape
    f = w_gate.shape[1]
    tm = _pick(t, (1024, 512, 256, 128, 64, 32, 16, 8))
    tn = _pick(f, (256, 128))
    assert tm >= CONV_W - 1
    n_t = t // tm
    wspec = pl.BlockSpec((d, tn), lambda j, bi, ti: (0, j))
    return pl.pallas_call(
        functools.partial(_ffn_in_seq_kernel, tm=tm, n_t=n_t),
        grid=(f // tn, b, n_t),
        in_specs=[pl.BlockSpec((None, tm, d), lambda j, bi, ti: (bi, ti, 0)),
                  wspec, wspec,
                  pl.BlockSpec((CONV_W, tn), lambda j, bi, ti: (0, j)),
                  pl.BlockSpec((1, tn), lambda j, bi, ti: (0, j)),
                  pl.BlockSpec((None, CONV_W - 1, tn), lambda j, bi, ti: (bi, 0, j))],
        out_specs=[pl.BlockSpec((None, tm, tn), lambda j, bi, ti: (bi, ti, j)),
                   pl.BlockSpec((None, CONV_W - 1, tn), lambda j, bi, ti: (bi, 0, j))],
        out_shape=[jax.ShapeDtypeStruct((b, t, f), BF16),
                   jax.ShapeDtypeStruct((b, CONV_W - 1, f), F32)],
        scratch_shapes=[pltpu.VMEM((CONV_W - 1, tn), F32)],
        compiler_params=_params(("parallel", "parallel", "arbitrary")),
        name="ffn_in_seq",
    )(xn, w_gate, w_up, conv_w, conv_b.reshape(1, f), conv_prev)


def _ffn_in_step_kernel(x_ref, wg_ref, wu_ref, cw_ref, cb_ref, p0_ref, p1_ref, h_ref, gt_ref):
    x = x_ref[...]
    gt = _dot(x, wg_ref[...])
    up = _dot(x, wu_ref[...])
    acc = (cb_ref[...] + p0_ref[...] * cw_ref[0:1, :] + p1_ref[...] * cw_ref[1:2, :]
           + gt * cw_ref[2:3, :])
    h_ref[...] = (acc * _sigmoid(acc) * up).astype(h_ref.dtype)
    gt_ref[...] = gt


def _ffn_in_step(xn, w_gate, w_up, conv_w, conv_b, prev0, prev1):
    m, d = xn.shape
    f = w_gate.shape[1]
    tn = _pick(f, (256, 128))
    wspec = pl.BlockSpec((d, tn), lambda j: (0, j))
    rows = pl.BlockSpec((m, tn), lambda j: (0, j))
    return pl.pallas_call(
        _ffn_in_step_kernel,
        grid=(f // tn,),
        in_specs=[pl.BlockSpec((m, d), lambda j: (0, 0)), wspec, wspec,
                  pl.BlockSpec((CONV_W, tn), lambda j: (0, j)),
                  pl.BlockSpec((1, tn), lambda j: (0, j)), rows, rows],
        out_specs=[rows, rows],
        out_shape=[jax.ShapeDtypeStruct((m, f), BF16), jax.ShapeDtypeStruct((m, f), F32)],
        compiler_params=_params(("parallel",)),
        name="ffn_in_step",
    )(xn, w_gate, w_up, conv_w, conv_b.reshape(1, f), prev0, prev1)


def _layout(d_model, rw_cols):
    rw = d_model // 2
    mb = d_model // 2
    lora = rw_cols - 3 * rw
    lp = -(-lora // LANE) * LANE
    off_q = 3 * rw
    off_ga = off_q + 3 * mb
    off_l = off_ga + 2 * d_model
    return dict(rw=rw, mb=mb, lora=lora, lp=lp, off_q=off_q, off_mk=off_q + mb, off_mv=off_q + 2 * mb,
                off_ga=off_ga, off_gb=off_ga + d_model, off_l=off_l, np=off_l + lp)


def _permute_cols(z, lay):
    rw3, lora, lp = 3 * lay["rw"], lay["lora"], lay["lp"]
    pad = jnp.zeros(z.shape[:-1] + (lp - lora,), z.dtype)
    return jnp.concatenate([z[..., :rw3], z[..., rw3 + lora:], z[..., rw3:rw3 + lora], pad], axis=-1)


def _split_prev(prev, lay):
    rw, lora, lp = lay["rw"], lay["lora"], lay["lp"]
    b = prev.shape[0]
    parts = [prev[:, i * rw:(i + 1) * rw] for i in range(3)]
    parts.append(jnp.pad(prev[:, 3 * rw:3 * rw + lora], ((0, 0), (0, lp - lora))))
    return [p.reshape(b, 1, -1) for p in parts]


def _unpermute_last(proj_last, lay):
    rw3, lora = 3 * lay["rw"], lay["lora"]
    return jnp.concatenate([proj_last[:, :rw3], proj_last[:, lay["off_l"]:lay["off_l"] + lora]], axis=-1)


def _rwkv_params(lay, rw_mu, rw_w0, rw_w2, rw_a0, rw_a2, rw_g2, rw_k_k, rw_k_a, rw_r_k, rw_ln_w, rw_ln_b):
    rw, lora, lp = lay["rw"], lay["lora"], lay["lp"]
    dw, da = rw_w2.shape[0], rw_a2.shape[0]
    row = lambda x: x.reshape(1, -1)

    def padded(w, start):
        return jnp.zeros((lp, rw), BF16).at[start:start + w.shape[0]].set(w.astype(BF16))

    return dict(
        mu_r=row(rw_mu[:rw]), mu_k=row(rw_mu[rw:2 * rw]), mu_v=row(rw_mu[2 * rw:3 * rw]),
        mu_l=row(jnp.pad(rw_mu[3 * rw:], (0, lp - lora))),
        w0=row(rw_w0), a0=row(rw_a0), k_k=row(rw_k_k), k_a=row(rw_k_a),
        w2p=padded(rw_w2, 0), a2p=padded(rw_a2, dw), g2p=padded(rw_g2, dw + da),
        r_k=row(rw_r_k), ln_w=row(rw_ln_w), ln_b=row(rw_ln_b))


def _mix_and_project(x2d, proj2d, o_a, o_b, w, lay):
    merged = _merge(o_a, o_b, w["w_a_out"], w["w_b_out"], proj2d, lay)
    return _matmul(merged, w["w_o"], res=x2d)


def _after_mix(x1, mk, mv, b, t, w):
    xn = _rmsnorm(x1, w["norm_x"], BF16)
    q = _matmul(xn, w["wx_q"], out_dtype=BF16)
    o = _xattn(q.reshape(b, t, -1), mk, mv)
    return _matmul(o.reshape(b * t, -1), w["wx_o"], res=x1)


def kernel(x_prompt, x_sample, mem_prompt, cache_moba_k, cache_moba_v, page_table, cache_mem_k, cache_mem_v, state_rwkv_wkv, state_rwkv_shift, state_ffn_conv, norm_mix, w_in, rw_mu, rw_w0, rw_w2, rw_a0, rw_a2, rw_g2, rw_k_k, rw_k_a, rw_r_k, rw_ln_w, rw_ln_b, w_a_out, w_b_out, w_o, norm_x, norm_mem, wx_q, wx_k, wx_v, wx_o, norm_ffn, w_gate, w_up, conv_w, conv_b, w_down, norm_out):
    assert w_in.shape[0] == 1, "single-layer model"
    bp, t, d = x_prompt.shape
    db, ts, _ = x_sample.shape
    assert ts == 1, "decode group advances one token"
    rw_cols = state_rwkv_shift.shape[-1]
    lay = _layout(d, rw_cols)
    rw, mbw = lay["rw"], lay["mb"]
    nh_rw, nh_mb = rw // RW_HEAD, mbw // MB_HEAD
    f = w_gate.shape[-1]
    n_mem = mem_prompt.shape[1]
    xw = wx_q.shape[-1]
    n_pages = page_table.shape[1]
    page = cache_moba_k.shape[2]
    assert page == PAGE_SIZE and n_pages % (MB_BLOCK // page) == 0
    past_len = n_pages * page

    w = dict(w_a_out=w_a_out[0], w_b_out=w_b_out[0], w_o=w_o[0], norm_x=norm_x[0],
             wx_q=wx_q[0], wx_o=wx_o[0])
    w_in_p = _permute_cols(w_in[0], lay).astype(BF16)
    prm = _rwkv_params(lay, rw_mu[0], rw_w0[0], rw_w2[0], rw_a0[0], rw_a2[0], rw_g2[0],
                       rw_k_k[0], rw_k_a[0], rw_r_k[0], rw_ln_w[0], rw_ln_b[0])
    nk_down = 2 if (f // 2) % LANE == 0 else 1

    xp = x_prompt.reshape(bp * t, d)
    proj_p = _matmul(_rmsnorm(xp, norm_mix[0], BF16), w_in_p)
    proj_p3 = proj_p.reshape(bp, t, lay["np"])
    zeros_prev = [jnp.zeros((bp, 1, wd), F32) for wd in (rw, rw, rw, lay["lp"])]
    streams = _rwkv_prep(proj_p3, zeros_prev, lay, prm)
    o_a_p, sw_p = _rwkv_scan(streams, jnp.zeros((bp, nh_rw, RW_HEAD, RW_HEAD), F32), prm,
                             min(SCAN_CHUNK, t))
    k_rot_p, o_b_p = _moba_prompt(proj_p3, lay)
    x1_p = _mix_and_project(xp, proj_p, o_a_p.reshape(bp * t, rw), o_b_p.reshape(bp * t, mbw), w, lay)
    mem_n = _rmsnorm(mem_prompt.reshape(bp * n_mem, d), norm_mem[0], BF16)
    mk_p = _matmul(mem_n, wx_k[0])
    mv_p = _matmul(mem_n, wx_v[0])
    x2_p = _after_mix(x1_p, mk_p.reshape(bp, n_mem, xw), mv_p.reshape(bp, n_mem, xw), bp, t, w)
    hid_p, cv_p = _ffn_in_seq(_rmsnorm(x2_p, norm_ffn[0], BF16).reshape(bp, t, d), w_gate[0], w_up[0],
                              conv_w[0], conv_b[0], jnp.zeros((bp, CONV_W - 1, f), F32))
    x3_p = _matmul(hid_p.reshape(bp * t, f), w_down[0], res=x2_p, nk=nk_down)
    y_p = _rmsnorm(x3_p, norm_out, F32).reshape(bp, t, d)

    xs = x_sample.reshape(db, d)
    proj_s = _matmul(_rmsnorm(xs, norm_mix[0], BF16), w_in_p)
    streams_s = _rwkv_prep(proj_s.reshape(db, 1, lay["np"]), _split_prev(state_rwkv_shift[0], lay),
                           lay, prm)
    streams_s = [jnp.pad(s, ((0, 0), (0, STEP_CHUNK - 1), (0, 0))) for s in streams_s]
    o_a_s, sw_s = _rwkv_scan(streams_s, state_rwkv_wkv[0], prm, STEP_CHUNK)
    o_a_s = o_a_s[:, 0, :]
    q_s, k_rot_s = _rope_rows(proj_s, lay, past_len)
    v_s = proj_s[:, lay["off_mv"]:lay["off_mv"] + mbw]
    heads3 = lambda z: z.reshape(db, nh_mb, MB_HEAD)
    block_sums = _block_sums(cache_moba_k[0], page_table)
    sel = _decode_select(heads3(q_s), block_sums)
    sel = sel.reshape(db, MB_TOPK, nh_mb).transpose(0, 2, 1)
    n_pool = cache_moba_k.shape[1]
    o_b_s = _decode_attend(heads3(q_s), heads3(k_rot_s), heads3(v_s), sel, page_table,
                           cache_moba_k[0].reshape(n_pool, page, mbw),
                           cache_moba_v[0].reshape(n_pool, page, mbw))
    x1_s = _mix_and_project(xs, proj_s, o_a_s, o_b_s.astype(BF16), w, lay)
    x2_s = _after_mix(x1_s, cache_mem_k[0].reshape(db, n_mem, xw), cache_mem_v[0].reshape(db, n_mem, xw),
                      db, 1, w)
    conv_prev_s = state_ffn_conv[0]
    hid_s, gt_s = _ffn_in_step(_rmsnorm(x2_s, norm_ffn[0], BF16), w_gate[0], w_up[0], conv_w[0],
                               conv_b[0], conv_prev_s[:, 0], conv_prev_s[:, 1])
    x3_s = _matmul(hid_s, w_down[0], res=x2_s, nk=nk_down)
    y_s = _rmsnorm(x3_s, norm_out, F32).reshape(db, 1, d)
    cv_s = jnp.stack([conv_prev_s[:, 1], gt_s], axis=1)

    v_p = proj_p3[:, :, lay["off_mv"]:lay["off_mv"] + mbw]
    return (y_p, y_s,
            k_rot_p.reshape(1, bp, t, nh_mb, MB_HEAD), v_p.reshape(1, bp, t, nh_mb, MB_HEAD),
            k_rot_s.reshape(1, db, 1, nh_mb, MB_HEAD), v_s.reshape(1, db, 1, nh_mb, MB_HEAD),
            mk_p.reshape(1, bp, n_mem, xw // X_HEAD, X_HEAD), mv_p.reshape(1, bp, n_mem, xw // X_HEAD, X_HEAD),
            sw_p[None], sw_s[None],
            _unpermute_last(proj_p3[:, t - 1], lay)[None], _unpermute_last(proj_s, lay)[None],
            cv_p[None], cv_s[None])
```

```python
import functools
import math

import jax
import jax.numpy as jnp
from jax import lax
from jax.experimental import pallas as pl
from jax.experimental.pallas import tpu as pltpu

F32 = jnp.float32
BF16 = jnp.bfloat16

NORM_EPS = 1e-5
RW_HEAD = 64
RW_GN_EPS = 64e-5
MB_HEAD = 128
MB_BLOCK = 256
MB_TOPK = 3
PAGE_SIZE = 128
ROPE_DIM = MB_HEAD // 4
ROPE_THETA = 500000.0
X_HEAD = 128
CONV_W = 3

LANE = 128
VMEM_LIMIT = 56 * 1024 * 1024
SCAN_CHUNK = 64
SCAN_HEADS = 16
STEP_CHUNK = 8


def _params(sem):
    return pltpu.CompilerParams(dimension_semantics=sem, vmem_limit_bytes=VMEM_LIMIT)


def _pick(n, prefs):
    for p in prefs:
        if n % p == 0:
            return p
    return n


def _dot(a, b, dims=(((1,), (0,)), ((), ())), exact=False):
    if exact:
        return lax.dot_general(a, b, dims, precision=lax.Precision.HIGHEST,
                               preferred_element_type=F32)
    return lax.dot_general(a.astype(BF16), b.astype(BF16), dims, preferred_element_type=F32)


_NT = (((1,), (1,)), ((), ()))
_TN = (((0,), (0,)), ((), ()))


def _sigmoid(x):
    return 1.0 / (1.0 + jnp.exp(-x))


def _rmsnorm_kernel(x_ref, g_ref, o_ref):
    x = x_ref[...]
    y = x * lax.rsqrt(jnp.mean(x * x, axis=-1, keepdims=True) + NORM_EPS)
    o_ref[...] = (y * g_ref[...]).astype(o_ref.dtype)


def _rmsnorm(x, g, out_dtype):
    m, d = x.shape
    tr = _pick(m, (256, 128, 64, 32, 16, 8))
    return pl.pallas_call(
        _rmsnorm_kernel,
        grid=(m // tr,),
        in_specs=[pl.BlockSpec((tr, d), lambda i: (i, 0)),
                  pl.BlockSpec((1, d), lambda i: (0, 0))],
        out_specs=pl.BlockSpec((tr, d), lambda i: (i, 0)),
        out_shape=jax.ShapeDtypeStruct((m, d), out_dtype),
        compiler_params=_params(("parallel",)),
        name="rmsnorm",
    )(x, g.reshape(1, d))


def _mm_kernel(a_ref, w_ref, *refs):
    o_ref = refs[-1]
    out = _dot(a_ref[...], w_ref[...])
    if len(refs) == 2:
        out = out + refs[0][...]
    o_ref[...] = out.astype(o_ref.dtype)


def _matmul(a, w, res=None, out_dtype=F32, tm=None, tn=256):
    m, kdim = a.shape
    n = w.shape[1]
    tm = tm or _pick(m, (1024, 512, 256, 128, 64, 32, 16, 8))
    tn = min(tn, n)
    assert m % tm == 0
    in_specs = [pl.BlockSpec((tm, kdim), lambda i, j: (i, 0)),
                pl.BlockSpec((kdim, tn), lambda i, j: (0, j))]
    args = [a, w]
    if res is not None:
        in_specs.append(pl.BlockSpec((tm, tn), lambda i, j: (i, j)))
        args.append(res)
    return pl.pallas_call(
        _mm_kernel,
        grid=(m // tm, pl.cdiv(n, tn)),
        in_specs=in_specs,
        out_specs=pl.BlockSpec((tm, tn), lambda i, j: (i, j)),
        out_shape=jax.ShapeDtypeStruct((m, n), out_dtype),
        compiler_params=_params(("parallel", "parallel")),
        name="matmul",
    )(*args)


def _prep_kernel(zr_ref, zk_ref, zv_ref, zl_ref, pr_ref, pk_ref, pv_ref, pl_ref,
                 mur_ref, muk_ref, muv_ref, mul_ref, w0_ref, a0_ref, kk_ref, ka_ref,
                 w2_ref, a2_ref, g2_ref,
                 r_ref, lw_ref, k_ref, v_ref, kkr_ref, a_ref, g_ref,
                 cr, ck, cv, cl, *, tp):
    t = pl.program_id(1)

    @pl.when(t == 0)
    def _():
        cr[...] = pr_ref[...]
        ck[...] = pk_ref[...]
        cv[...] = pv_ref[...]
        cl[...] = pl_ref[...]

    def shift_mix(z_ref, carry, mu_ref):
        z = z_ref[...]
        if tp == 1:
            prev = carry[...]
        else:
            first = lax.broadcasted_iota(jnp.int32, z.shape, 0) == 0
            prev = jnp.where(first, carry[...], pltpu.roll(z, 1, 0))
        carry[...] = z[tp - 1:tp, :]
        return z + mu_ref[...] * (prev - z)

    r = shift_mix(zr_ref, cr, mur_ref)
    k = shift_mix(zk_ref, ck, muk_ref)
    v = shift_mix(zv_ref, cv, muv_ref)
    xl = shift_mix(zl_ref, cl, mul_ref)
    u = w0_ref[...] + _dot(jnp.tanh(xl), w2_ref[...])
    lw = -math.exp(-0.5) * _sigmoid(u)
    a = _sigmoid(a0_ref[...] + _dot(xl, a2_ref[...]))
    g = _dot(_sigmoid(xl), g2_ref[...])
    r_ref[...] = r
    lw_ref[...] = lw
    k_ref[...] = k * (1.0 + (a - 1.0) * ka_ref[...])
    v_ref[...] = v
    kkr_ref[...] = k * kk_ref[...]
    a_ref[...] = a
    g_ref[...] = g


def _rwkv_prep(proj, prev, lay, prm):
    b, t, _ = proj.shape
    rw, lp = lay["rw"], lay["lp"]
    tp = _pick(t, (128, 64, 32, 16, 8))
    assert lay["off_l"] % lp == 0
    lblk = lay["off_l"] // lp
    row = lambda j: pl.BlockSpec((None, tp, rw), lambda bi, ti, j=j: (bi, ti, j))
    prev_spec = lambda w: pl.BlockSpec((None, 1, w), lambda bi, ti: (bi, 0, 0))
    vec = lambda w: pl.BlockSpec((1, w), lambda bi, ti: (0, 0))
    mat = pl.BlockSpec((lp, rw), lambda bi, ti: (0, 0))
    out_spec = pl.BlockSpec((None, tp, rw), lambda bi, ti: (bi, ti, 0))
    out_sds = jax.ShapeDtypeStruct((b, t, rw), F32)
    return pl.pallas_call(
        functools.partial(_prep_kernel, tp=tp),
        grid=(b, t // tp),
        in_specs=[row(0), row(1), row(2),
                  pl.BlockSpec((None, tp, lp), lambda bi, ti: (bi, ti, lblk)),
                  prev_spec(rw), prev_spec(rw), prev_spec(rw), prev_spec(lp),
                  vec(rw), vec(rw), vec(rw), vec(lp), vec(rw), vec(rw), vec(rw), vec(rw),
                  mat, mat, mat],
        out_specs=[out_spec] * 7,
        out_shape=[out_sds] * 7,
        scratch_shapes=[pltpu.VMEM((1, rw), F32), pltpu.VMEM((1, rw), F32),
                        pltpu.VMEM((1, rw), F32), pltpu.VMEM((1, lp), F32)],
        compiler_params=_params(("parallel", "arbitrary")),
        name="rwkv_prep",
    )(proj, proj, proj, proj, *prev,
      prm["mu_r"], prm["mu_k"], prm["mu_v"], prm["mu_l"], prm["w0"], prm["a0"],
      prm["k_k"], prm["k_a"], prm["w2p"], prm["a2p"], prm["g2p"])


def _scan_kernel(r_ref, lw_ref, k_ref, v_ref, kkr_ref, a_ref, g_ref, rk_ref, lnw_ref, lnb_ref,
                 s0_ref, o_ref, sf_ref, s_scr, *, chunk, heads, n_chunks):
    c = pl.program_id(2)

    @pl.when(c == 0)
    def _():
        s_scr[...] = s0_ref[...]

    n = chunk
    row = lax.broadcasted_iota(jnp.int32, (n, n), 0)
    col = lax.broadcasted_iota(jnp.int32, (n, n), 1)
    incl = row >= col
    strict = row > col
    eye = (row == col).astype(F32)

    lw = lw_ref[...]
    cum = _dot(incl.astype(F32), lw, exact=True)
    cum_prev = cum - lw
    total = cum[n - 1:n, :]
    e_cum = jnp.exp(cum)
    e_prev = jnp.exp(cum_prev)
    e_neg = jnp.exp(-cum)
    e_rest = jnp.exp(total - cum)
    e_total = jnp.exp(total)

    r_all, k_all, v_all = r_ref[...], k_ref[...], v_ref[...]
    kkr_all, a_all, g_all = kkr_ref[...], a_ref[...], g_ref[...]
    rk_all, lnw_all, lnb_all = rk_ref[...], lnw_ref[...], lnb_ref[...]

    hs = range(heads)
    sls = [slice(h * RW_HEAD, (h + 1) * RW_HEAD) for h in hs]
    r = [r_all[:, sl] for sl in sls]
    k = [k_all[:, sl] for sl in sls]
    v = [v_all[:, sl] for sl in sls]
    kk, bb, q2 = [], [], []
    for h, sl in enumerate(sls):
        kkr = kkr_all[:, sl]
        kk_h = kkr / jnp.maximum(jnp.sqrt(jnp.sum(kkr * kkr, axis=-1, keepdims=True)), 1e-12)
        kk.append(kk_h)
        bb.append(kk_h * a_all[:, sl])
        q2.append(jnp.concatenate([kk_h * e_prev[:, sl], r[h] * e_cum[:, sl]], axis=0))
    p_k = [_dot(q2[h], k[h] * e_neg[:, sls[h]], _NT) for h in hs]
    p_b = [_dot(q2[h], bb[h] * e_neg[:, sls[h]], _NT) for h in hs]
    s = [s_scr[h] for h in hs]
    qs = [_dot(q2[h], s[h], _NT) for h in hs]
    m_k = [jnp.where(strict, p[:n], 0.0) for p in p_k]
    a_k = [jnp.where(incl, p[n:], 0.0) for p in p_k]
    a_b = [jnp.where(incl, p[n:], 0.0) for p in p_b]
    npow = [jnp.where(strict, -p[:n], 0.0) for p in p_b]
    tinv = [eye + x for x in npow]
    for _ in range(int(math.log2(n)) - 1):
        npow = [_dot(x, x) for x in npow]
        tinv = [t + _dot(t, x) for t, x in zip(tinv, npow)]
    rhs = [qs[h][:n] + _dot(m_k[h], v[h]) for h in hs]
    u = [_dot(tinv[h], rhs[h]) for h in hs]
    y = [qs[h][n:] + _dot(a_k[h], v[h]) - _dot(a_b[h], u[h]) for h in hs]
    for h, sl in enumerate(sls):
        s_scr[h] = (s[h] * e_total[:, sl] + _dot(v[h], k[h] * e_rest[:, sl], _TN)
                    - _dot(u[h], bb[h] * e_rest[:, sl], _TN))
    outs = []
    for h, sl in enumerate(sls):
        mu = jnp.mean(y[h], axis=-1, keepdims=True)
        yc = y[h] - mu
        var = jnp.mean(yc * yc, axis=-1, keepdims=True)
        yn = yc * lax.rsqrt(var + RW_GN_EPS) * lnw_all[:, sl] + lnb_all[:, sl]
        bonus = jnp.sum(r[h] * k[h] * rk_all[:, sl], axis=-1, keepdims=True) * v[h]
        outs.append((yn + bonus) * g_all[:, sl])
    o_ref[...] = jnp.concatenate(outs, axis=1).astype(o_ref.dtype)

    @pl.when(c == n_chunks - 1)
    def _():
        sf_ref[...] = s_scr[...]


def _rwkv_scan(streams, s0, prm, chunk):
    b, t, rw = streams[0].shape
    nh = rw // RW_HEAD
    hg = _pick(nh, (SCAN_HEADS, 4, 2))
    w = hg * RW_HEAD
    assert t % chunk == 0 and chunk & (chunk - 1) == 0 and w % LANE == 0
    n_chunks = t // chunk
    stream_spec = pl.BlockSpec((None, chunk, w), lambda bi, gi, ci: (bi, ci, gi))
    vec_spec = pl.BlockSpec((1, w), lambda bi, gi, ci: (0, gi))
    state_spec = pl.BlockSpec((None, hg, RW_HEAD, RW_HEAD), lambda bi, gi, ci: (bi, gi, 0, 0))
    return pl.pallas_call(
        functools.partial(_scan_kernel, chunk=chunk, heads=hg, n_chunks=n_chunks),
        grid=(b, nh // hg, n_chunks),
        in_specs=[stream_spec] * 7 + [vec_spec] * 3 + [state_spec],
        out_specs=[stream_spec, state_spec],
        out_shape=[jax.ShapeDtypeStruct((b, t, rw), BF16),
                   jax.ShapeDtypeStruct((b, nh, RW_HEAD, RW_HEAD), F32)],
        scratch_shapes=[pltpu.VMEM((hg, RW_HEAD, RW_HEAD), F32)],
        compiler_params=_params(("parallel", "parallel", "arbitrary")),
        name="rwkv_scan",
    )(*streams, prm["r_k"], prm["ln_w"], prm["ln_b"], s0)


def _rope_tables(pos):
    half = ROPE_DIM // 2
    inv = 1.0 / (ROPE_THETA ** (jnp.arange(0, ROPE_DIM, 2, dtype=F32) / ROPE_DIM))
    ang = pos.astype(F32)[:, None] * inv[None, :]
    cos, sin = jnp.cos(ang), jnp.sin(ang)
    t = pos.shape[0]
    rest = MB_HEAD - ROPE_DIM
    c = jnp.concatenate([cos, cos, jnp.ones((t, rest), F32)], axis=1)
    s = jnp.concatenate([-sin, sin, jnp.zeros((t, rest), F32)], axis=1)
    del half
    return c, s


def _rope(x, c, s):
    half = ROPE_DIM // 2
    lane = lax.broadcasted_iota(jnp.int32, x.shape, 1)
    swapped = jnp.where(lane < half, pltpu.roll(x, MB_HEAD - half, 1), pltpu.roll(x, half, 1))
    return x * c + swapped * s


def _moba_prompt_kernel(q_ref, k_ref, v_ref, c_ref, s_ref, kout_hbm, vout_hbm, o_ref,
                        kbuf, vbuf, sems, *, nb):
    bi, hi = pl.program_id(0), pl.program_id(1)
    c, s = c_ref[...], s_ref[...]
    q = _rope(q_ref[...], c, s)
    k = _rope(k_ref[...], c, s)
    kbuf[...] = k
    vbuf[...] = v_ref[...]
    k_copy = pltpu.make_async_copy(kbuf, kout_hbm.at[bi, :, hi, :], sems.at[0])
    v_copy = pltpu.make_async_copy(vbuf, vout_hbm.at[bi, :, hi, :], sems.at[1])
    k_copy.start()
    v_copy.start()
    t = q.shape[0]
    k_means = jnp.sum(k.reshape(nb, MB_BLOCK, MB_HEAD), axis=1) * (1.0 / MB_BLOCK)
    scores = _dot(q, k_means, _NT, exact=True)
    qb = q.astype(BF16)
    kb = k.astype(BF16)
    vb = v_ref[...].astype(BF16)
    scale = MB_HEAD ** -0.5
    rowi = lax.broadcasted_iota(jnp.int32, (MB_BLOCK, MB_BLOCK), 0)
    coli = lax.broadcasted_iota(jnp.int32, (MB_BLOCK, MB_BLOCK), 1)
    causal = rowi >= coli
    neg_inf = float("-inf")
    for i in range(nb):
        rows = slice(i * MB_BLOCK, (i + 1) * MB_BLOCK)
        q_i = qb[rows]
        sc = scores[rows]
        logits = []
        for j in range(i + 1):
            cols = slice(j * MB_BLOCK, (j + 1) * MB_BLOCK)
            lg = _dot(q_i, kb[cols], _NT) * scale
            if j == i:
                mask = causal
            elif i <= MB_TOPK:
                mask = None
            else:
                sj = sc[:, j:j + 1]
                rank = jnp.zeros((MB_BLOCK, 1), jnp.int32)
                for m in range(i):
                    if m == j:
                        continue
                    sm = sc[:, m:m + 1]
                    ahead = (sm > sj) | (sm == sj) if m < j else (sm > sj)
                    rank = rank + ahead.astype(jnp.int32)
                mask = rank < MB_TOPK
            if mask is not None:
                lg = jnp.where(mask, lg, neg_inf)
            logits.append(lg)
        mx = logits[0].max(axis=-1, keepdims=True)
        for lg in logits[1:]:
            mx = jnp.maximum(mx, lg.max(axis=-1, keepdims=True))
        den = jnp.zeros((MB_BLOCK, 1), F32)
        acc = jnp.zeros((MB_BLOCK, MB_HEAD), F32)
        for j, lg in enumerate(logits):
            pj = jnp.exp(lg - mx)
            den = den + jnp.sum(pj, axis=-1, keepdims=True)
            acc = acc + _dot(pj, vb[j * MB_BLOCK:(j + 1) * MB_BLOCK])
        o_ref[rows, :] = (acc / den).astype(o_ref.dtype)
    del t
    k_copy.wait()
    v_copy.wait()


def _moba_prompt(proj, lay):
    b, t, _ = proj.shape
    assert t % MB_BLOCK == 0
    mbw = lay["mb"]
    nh = mbw // MB_HEAD
    ctab, stab = _rope_tables(jnp.arange(t))
    qb, kb, vb = lay["off_q"] // MB_HEAD, lay["off_mk"] // MB_HEAD, lay["off_mv"] // MB_HEAD
    head = lambda off: pl.BlockSpec((None, t, MB_HEAD), lambda bi, hi, off=off: (bi, 0, off + hi))
    tab = pl.BlockSpec((t, MB_HEAD), lambda bi, hi: (0, 0))
    hbm = pl.BlockSpec(memory_space=pl.ANY)
    rows_sds = jax.ShapeDtypeStruct((b, t, nh, MB_HEAD), F32)
    return pl.pallas_call(
        functools.partial(_moba_prompt_kernel, nb=t // MB_BLOCK),
        grid=(b, nh),
        in_specs=[head(qb), head(kb), head(vb), tab, tab],
        out_specs=[hbm, hbm, pl.BlockSpec((None, t, MB_HEAD), lambda bi, hi: (bi, 0, hi))],
        out_shape=[rows_sds, rows_sds, jax.ShapeDtypeStruct((b, t, mbw), BF16)],
        scratch_shapes=[pltpu.VMEM((t, MB_HEAD), F32), pltpu.VMEM((t, MB_HEAD), F32),
                        pltpu.SemaphoreType.DMA((2,))],
        compiler_params=_params(("parallel", "parallel")),
        name="moba_prompt",
    )(proj, proj, proj, ctab, stab)


def _rope_rows_kernel(q_ref, k_ref, c_ref, s_ref, qo_ref, ko_ref, *, heads):
    c, s = c_ref[...], s_ref[...]
    for h in range(heads):
        sl = slice(h * MB_HEAD, (h + 1) * MB_HEAD)
        qo_ref[:, sl] = _rope(q_ref[:, sl], c, s)
        ko_ref[:, sl] = _rope(k_ref[:, sl], c, s)


def _rope_rows(proj2d, lay, pos):
    m = proj2d.shape[0]
    mbw = lay["mb"]
    ctab, stab = _rope_tables(jnp.full((1,), pos))
    assert lay["off_q"] % mbw == 0 and lay["off_mk"] % mbw == 0
    spec = lambda off: pl.BlockSpec((m, mbw), lambda i, off=off: (0, off // mbw))
    tab = pl.BlockSpec((1, MB_HEAD), lambda i: (0, 0))
    out = pl.BlockSpec((m, mbw), lambda i: (0, 0))
    sds = jax.ShapeDtypeStruct((m, mbw), F32)
    return pl.pallas_call(
        functools.partial(_rope_rows_kernel, heads=mbw // MB_HEAD),
        grid=(1,),
        in_specs=[spec(lay["off_q"]), spec(lay["off_mk"]), tab, tab],
        out_specs=[out, out],
        out_shape=[sds, sds],
        compiler_params=_params(("arbitrary",)),
        name="rope_rows",
    )(proj2d, proj2d, ctab, stab)


def _block_sum_kernel(pt_ref, *refs, bps, ppb):
    page_refs, o_ref = refs[:-1], refs[-1]
    for blk in range(bps):
        acc = jnp.sum(page_refs[blk * ppb][...], axis=0)
        for j in range(1, ppb):
            acc = acc + jnp.sum(page_refs[blk * ppb + j][...], axis=0)
        o_ref[blk] = acc


def _block_sums(cache_k, page_table):
    _, page, nh, d = cache_k.shape
    db, n_pages = page_table.shape
    ppb = MB_BLOCK // page
    nb = n_pages // ppb
    bps = _pick(nb, (4, 2))
    n_in = bps * ppb

    def page_spec(s):
        return pl.BlockSpec((None, page, nh, d), lambda bi, ni, pt: (pt[bi, ni * n_in + s], 0, 0, 0))

    return pl.pallas_call(
        functools.partial(_block_sum_kernel, bps=bps, ppb=ppb),
        grid_spec=pltpu.PrefetchScalarGridSpec(
            num_scalar_prefetch=1,
            grid=(db, nb // bps),
            in_specs=[page_spec(s) for s in range(n_in)],
            out_specs=pl.BlockSpec((None, bps, nh, d), lambda bi, ni, pt: (bi, ni, 0, 0)),
        ),
        out_shape=jax.ShapeDtypeStruct((db, nb, nh, d), F32),
        compiler_params=_params(("parallel", "parallel")),
        name="moba_block_sums",
    )(page_table, *([cache_k] * n_in))


def _decode_select_kernel(q_ref, ks_ref, idx_ref, *, nb):
    q = q_ref[...]
    scores = jnp.sum(ks_ref[...] * (1.0 / MB_BLOCK) * q[None], axis=-1, keepdims=True)
    blk = lax.broadcasted_iota(jnp.int32, scores.shape, 0)
    rank = jnp.zeros(scores.shape, jnp.int32)
    for m in range(nb):
        sm = scores[m:m + 1]
        ahead = (sm > scores) | ((sm == scores) & (blk > m))
        rank = rank + ahead.astype(jnp.int32)
    for r in range(MB_TOPK):
        idx_ref[r] = jnp.sum(jnp.where(rank == r, blk, 0), axis=0)


def _decode_select(q, block_sums):
    db, nh, d = q.shape
    nb = block_sums.shape[1]
    assert nb >= MB_TOPK
    return pl.pallas_call(
        functools.partial(_decode_select_kernel, nb=nb),
        grid=(db,),
        in_specs=[pl.BlockSpec((None, nh, d), lambda bi: (bi, 0, 0)),
                  pl.BlockSpec((None, nb, nh, d), lambda bi: (bi, 0, 0, 0))],
        out_specs=pl.BlockSpec((None, MB_TOPK, nh, 1), lambda bi: (bi, 0, 0, 0)),
        out_shape=jax.ShapeDtypeStruct((db, MB_TOPK, nh, 1), jnp.int32),
        compiler_params=_params(("parallel",)),
        name="moba_decode_select",
    )(q, block_sums)


def _decode_attend_kernel(sel_ref, pt_ref, q_ref, kn_ref, vn_ref, ck_hbm, cv_hbm, o_ref,
                          kbuf, vbuf, sems, *, nh, ppb):
    bi = pl.program_id(0)
    n_sel = MB_TOPK * ppb
    page = kbuf.shape[2]

    def head_copies(h):
        copies = []
        for r in range(MB_TOPK):
            blk = sel_ref[(bi * nh + h) * MB_TOPK + r]
            for j in range(ppb):
                phys = pt_ref[bi, blk * ppb + j]
                s = r * ppb + j
                copies.append(pltpu.make_async_copy(ck_hbm.at[phys, :, h, :], kbuf.at[h, s], sems.at[h]))
                copies.append(pltpu.make_async_copy(cv_hbm.at[phys, :, h, :], vbuf.at[h, s], sems.at[h]))
        return copies

    all_copies = [head_copies(h) for h in range(nh)]
    for copies in all_copies:
        for cp in copies:
            cp.start()

    scale = MB_HEAD ** -0.5
    for h in range(nh):
        for cp in all_copies[h]:
            cp.wait()
        q = q_ref[h:h + 1, :]
        keys = kbuf[h].reshape(n_sel * page, MB_HEAD)
        vals = vbuf[h].reshape(n_sel * page, MB_HEAD)
        lg = _dot(jnp.broadcast_to(q, (8, MB_HEAD)), keys, _NT)[0:1] * scale
        own = jnp.sum(q * kn_ref[h:h + 1, :], axis=-1, keepdims=True) * scale
        mx = jnp.maximum(own, lg.max(axis=-1, keepdims=True))
        p_own = jnp.exp(own - mx)
        p = jnp.exp(lg - mx)
        den = p_own + jnp.sum(p, axis=-1, keepdims=True)
        acc = p_own * vn_ref[h:h + 1, :] + _dot(jnp.broadcast_to(p, (8, p.shape[1])), vals)[0:1]
        o_ref[h:h + 1, :] = acc / den


def _decode_attend(q, k_new, v_new, sel, page_table, cache_k, cache_v):
    db, nh, d = q.shape
    page = cache_k.shape[1]
    ppb = MB_BLOCK // page
    n_sel = MB_TOPK * ppb
    vec = pl.BlockSpec((None, nh, d), lambda bi, sel_r, pt_r: (bi, 0, 0))
    hbm = pl.BlockSpec(memory_space=pl.ANY)
    out = pl.pallas_call(
        functools.partial(_decode_attend_kernel, nh=nh, ppb=ppb),
        grid_spec=pltpu.PrefetchScalarGridSpec(
            num_scalar_prefetch=2,
            grid=(db,),
            in_specs=[vec, vec, vec, hbm, hbm],
            out_specs=vec,
            scratch_shapes=[pltpu.VMEM((nh, n_sel, page, d), F32),
                            pltpu.VMEM((nh, n_sel, page, d), F32),
                            pltpu.SemaphoreType.DMA((nh,))],
        ),
        out_shape=jax.ShapeDtypeStruct((db, nh, d), F32),
        compiler_params=_params(("arbitrary",)),
        name="moba_decode_attend",
    )(sel.reshape(-1), page_table, q, k_new, v_new, cache_k, cache_v)
    return out.reshape(db, nh * d)


def _merge_kernel(oa_ref, ob_ref, wa_ref, wb_ref, ga_ref, gb_ref, o_ref):
    ya = _dot(oa_ref[...], wa_ref[...])
    yb = _dot(ob_ref[...], wb_ref[...])
    o_ref[...] = (_sigmoid(ga_ref[...]) * ya + _sigmoid(gb_ref[...]) * yb).astype(o_ref.dtype)


def _merge(o_a, o_b, w_a, w_b, proj2d, lay):
    m, ka = o_a.shape
    kb = o_b.shape[1]
    d = w_a.shape[1]
    tm = _pick(m, (1024, 512, 256, 128, 64, 32, 16, 8))
    tn = _pick(d, (256, 128))
    assert lay["off_ga"] % tn == 0 and lay["off_gb"] % tn == 0
    ga, gb = lay["off_ga"] // tn, lay["off_gb"] // tn
    return pl.pallas_call(
        _merge_kernel,
        grid=(m // tm, d // tn),
        in_specs=[pl.BlockSpec((tm, ka), lambda i, j: (i, 0)),
                  pl.BlockSpec((tm, kb), lambda i, j: (i, 0)),
                  pl.BlockSpec((ka, tn), lambda i, j: (0, j)),
                  pl.BlockSpec((kb, tn), lambda i, j: (0, j)),
                  pl.BlockSpec((tm, tn), lambda i, j: (i, ga + j)),
                  pl.BlockSpec((tm, tn), lambda i, j: (i, gb + j))],
        out_specs=pl.BlockSpec((tm, tn), lambda i, j: (i, j)),
        out_shape=jax.ShapeDtypeStruct((m, d), BF16),
        compiler_params=_params(("parallel", "parallel")),
        name="gated_merge",
    )(o_a, o_b, w_a, w_b, proj2d, proj2d)


def _xattn_kernel(q_ref, k_ref, v_ref, o_ref, *, heads):
    scale = X_HEAD ** -0.5
    for h in range(heads):
        sl = slice(h * X_HEAD, (h + 1) * X_HEAD)
        q = q_ref[:, sl]
        if q.shape[0] < 8:
            q = jnp.broadcast_to(q, (8, X_HEAD))
        lg = _dot(q, k_ref[:, sl], _NT) * scale
        p = jnp.exp(lg - lg.max(axis=-1, keepdims=True))
        o = _dot(p, v_ref[:, sl]) / jnp.sum(p, axis=-1, keepdims=True)
        o_ref[:, sl] = o[:o_ref.shape[0]].astype(o_ref.dtype)


def _xattn(q, mk, mv):
    b, t, xw = q.shape
    n_mem = mk.shape[1]
    tq = _pick(t, (512, 256, 128, 64, 32, 16, 8))
    qspec = pl.BlockSpec((None, tq, xw), lambda bi, ti: (bi, ti, 0))
    mspec = pl.BlockSpec((None, n_mem, xw), lambda bi, ti: (bi, 0, 0))
    return pl.pallas_call(
        functools.partial(_xattn_kernel, heads=xw // X_HEAD),
        grid=(b, t // tq),
        in_specs=[qspec, mspec, mspec],
        out_specs=qspec,
        out_shape=jax.ShapeDtypeStruct((b, t, xw), BF16),
        compiler_params=_params(("parallel", "parallel")),
        name="cross_attention",
    )(q, mk, mv)


def _ffn_in_seq_kernel(x_ref, wg_ref, wu_ref, cw_ref, cb_ref, prev_ref, h_ref, cn_ref, carry,
                       wg_b, wu_b, *, tm, n_t):
    t = pl.program_id(2)

    @pl.when((pl.program_id(1) == 0) & (t == 0))
    def _():
        wg_b[...] = wg_ref[...].astype(BF16)
        wu_b[...] = wu_ref[...].astype(BF16)

    @pl.when(t == 0)
    def _():
        carry[...] = prev_ref[...]

    x = x_ref[...]
    gt = _dot(x, wg_b[...])
    up = _dot(x, wu_b[...])
    rowi = lax.broadcasted_iota(jnp.int32, gt.shape, 0)
    c0, c1 = carry[0:1, :], carry[1:2, :]
    g1 = jnp.where(rowi == 0, c1, pltpu.roll(gt, 1, 0))
    g2 = jnp.where(rowi == 0, c0, jnp.where(rowi == 1, c1, pltpu.roll(gt, 2, 0)))
    acc = cb_ref[...] + g2 * cw_ref[0:1, :] + g1 * cw_ref[1:2, :] + gt * cw_ref[2:3, :]
    h_ref[...] = (acc * _sigmoid(acc) * up).astype(h_ref.dtype)
    carry[...] = gt[tm - 2:tm, :]

    @pl.when(t == n_t - 1)
    def _():
        cn_ref[...] = gt[tm - 2:tm, :]


def _ffn_in_seq(xn, w_gate, w_up, conv_w, conv_b, conv_prev):
    b, t, d = xn.shape
    f = w_gate.shape[1]
    tm = _pick(t, (1024, 512, 256, 128, 64, 32, 16, 8))
    tn = _pick(f, (256, 128))
    assert tm >= CONV_W - 1
    n_t = t // tm
    wspec = pl.BlockSpec((d, tn), lambda j, bi, ti: (0, j))
    return pl.pallas_call(
        functools.partial(_ffn_in_seq_kernel, tm=tm, n_t=n_t),
        grid=(f // tn, b, n_t),
        in_specs=[pl.BlockSpec((None, tm, d), lambda j, bi, ti: (bi, ti, 0)),
                  wspec, wspec,
                  pl.BlockSpec((CONV_W, tn), lambda j, bi, ti: (0, j)),
                  pl.BlockSpec((1, tn), lambda j, bi, ti: (0, j)),
                  pl.BlockSpec((None, CONV_W - 1, tn), lambda j, bi, ti: (bi, 0, j))],
        out_specs=[pl.BlockSpec((None, tm, tn), lambda j, bi, ti: (bi, ti, j)),
                   pl.BlockSpec((None, CONV_W - 1, tn), lambda j, bi, ti: (bi, 0, j))],
        out_shape=[jax.ShapeDtypeStruct((b, t, f), BF16),
                   jax.ShapeDtypeStruct((b, CONV_W - 1, f), F32)],
        scratch_shapes=[pltpu.VMEM((CONV_W - 1, tn), F32),
                        pltpu.VMEM((d, tn), BF16), pltpu.VMEM((d, tn), BF16)],
        compiler_params=_params(("parallel", "arbitrary", "arbitrary")),
        name="ffn_in_seq",
    )(xn, w_gate, w_up, conv_w, conv_b.reshape(1, f), conv_prev)


def _ffn_in_step_kernel(x_ref, wg_ref, wu_ref, cw_ref, cb_ref, p0_ref, p1_ref, h_ref, gt_ref):
    x = x_ref[...]
    gt = _dot(x, wg_ref[...])
    up = _dot(x, wu_ref[...])
    acc = (cb_ref[...] + p0_ref[...] * cw_ref[0:1, :] + p1_ref[...] * cw_ref[1:2, :]
           + gt * cw_ref[2:3, :])
    h_ref[...] = (acc * _sigmoid(acc) * up).astype(h_ref.dtype)
    gt_ref[...] = gt


def _ffn_in_step(xn, w_gate, w_up, conv_w, conv_b, prev0, prev1):
    m, d = xn.shape
    f = w_gate.shape[1]
    tn = _pick(f, (256, 128))
    wspec = pl.BlockSpec((d, tn), lambda j: (0, j))
    rows = pl.BlockSpec((m, tn), lambda j: (0, j))
    return pl.pallas_call(
        _ffn_in_step_kernel,
        grid=(f // tn,),
        in_specs=[pl.BlockSpec((m, d), lambda j: (0, 0)), wspec, wspec,
                  pl.BlockSpec((CONV_W, tn), lambda j: (0, j)),
                  pl.BlockSpec((1, tn), lambda j: (0, j)), rows, rows],
        out_specs=[rows, rows],
        out_shape=[jax.ShapeDtypeStruct((m, f), BF16), jax.ShapeDtypeStruct((m, f), F32)],
        compiler_params=_params(("parallel",)),
        name="ffn_in_step",
    )(xn, w_gate, w_up, conv_w, conv_b.reshape(1, f), prev0, prev1)


def _layout(d_model, rw_cols):
    rw = d_model // 2
    mb = d_model // 2
    lora = rw_cols - 3 * rw
    lp = -(-lora // LANE) * LANE
    off_q = 3 * rw
    off_ga = off_q + 3 * mb
    off_l = off_ga + 2 * d_model
    return dict(rw=rw, mb=mb, lora=lora, lp=lp, off_q=off_q, off_mk=off_q + mb, off_mv=off_q + 2 * mb,
                off_ga=off_ga, off_gb=off_ga + d_model, off_l=off_l, np=off_l + lp)


def _permute_cols(z, lay):
    rw3, lora, lp = 3 * lay["rw"], lay["lora"], lay["lp"]
    pad = jnp.zeros(z.shape[:-1] + (lp - lora,), z.dtype)
    return jnp.concatenate([z[..., :rw3], z[..., rw3 + lora:], z[..., rw3:rw3 + lora], pad], axis=-1)


def _split_prev(prev, lay):
    rw, lora, lp = lay["rw"], lay["lora"], lay["lp"]
    b = prev.shape[0]
    parts = [prev[:, i * rw:(i + 1) * rw] for i in range(3)]
    parts.append(jnp.pad(prev[:, 3 * rw:3 * rw + lora], ((0, 0), (0, lp - lora))))
    return [p.reshape(b, 1, -1) for p in parts]


def _unpermute_last(proj_last, lay):
    rw3, lora = 3 * lay["rw"], lay["lora"]
    return jnp.concatenate([proj_last[:, :rw3], proj_last[:, lay["off_l"]:lay["off_l"] + lora]], axis=-1)


def _rwkv_params(lay, rw_mu, rw_w0, rw_w2, rw_a0, rw_a2, rw_g2, rw_k_k, rw_k_a, rw_r_k, rw_ln_w, rw_ln_b):
    rw, lora, lp = lay["rw"], lay["lora"], lay["lp"]
    dw, da = rw_w2.shape[0], rw_a2.shape[0]
    row = lambda x: x.reshape(1, -1)

    def padded(w, start):
        return jnp.zeros((lp, rw), BF16).at[start:start + w.shape[0]].set(w.astype(BF16))

    return dict(
        mu_r=row(rw_mu[:rw]), mu_k=row(rw_mu[rw:2 * rw]), mu_v=row(rw_mu[2 * rw:3 * rw]),
        mu_l=row(jnp.pad(rw_mu[3 * rw:], (0, lp - lora))),
        w0=row(rw_w0), a0=row(rw_a0), k_k=row(rw_k_k), k_a=row(rw_k_a),
        w2p=padded(rw_w2, 0), a2p=padded(rw_a2, dw), g2p=padded(rw_g2, dw + da),
        r_k=row(rw_r_k), ln_w=row(rw_ln_w), ln_b=row(rw_ln_b))


def _mix_and_project(x2d, proj2d, o_a, o_b, w, lay):
    merged = _merge(o_a, o_b, w["w_a_out"], w["w_b_out"], proj2d, lay)
    return _matmul(merged, w["w_o"], res=x2d)


def _after_mix(x1, mk, mv, b, t, w):
    xn = _rmsnorm(x1, w["norm_x"], BF16)
    q = _matmul(xn, w["wx_q"], out_dtype=BF16)
    o = _xattn(q.reshape(b, t, -1), mk, mv)
    return _matmul(o.reshape(b * t, -1), w["wx_o"], res=x1)


def kernel(x_prompt, x_sample, mem_prompt, cache_moba_k, cache_moba_v, page_table, cache_mem_k, cache_mem_v, state_rwkv_wkv, state_rwkv_shift, state_ffn_conv, norm_mix, w_in, rw_mu, rw_w0, rw_w2, rw_a0, rw_a2, rw_g2, rw_k_k, rw_k_a, rw_r_k, rw_ln_w, rw_ln_b, w_a_out, w_b_out, w_o, norm_x, norm_mem, wx_q, wx_k, wx_v, wx_o, norm_ffn, w_gate, w_up, conv_w, conv_b, w_down, norm_out):
    assert w_in.shape[0] == 1, "single-layer model"
    bp, t, d = x_prompt.shape
    db, ts, _ = x_sample.shape
    assert ts == 1, "decode group advances one token"
    rw_cols = state_rwkv_shift.shape[-1]
    lay = _layout(d, rw_cols)
    rw, mbw = lay["rw"], lay["mb"]
    nh_rw, nh_mb = rw // RW_HEAD, mbw // MB_HEAD
    f = w_gate.shape[-1]
    n_mem = mem_prompt.shape[1]
    xw = wx_q.shape[-1]
    n_pages = page_table.shape[1]
    page = cache_moba_k.shape[2]
    assert page == PAGE_SIZE and n_pages % (MB_BLOCK // page) == 0
    past_len = n_pages * page

    w = dict(w_a_out=w_a_out[0], w_b_out=w_b_out[0], w_o=w_o[0], norm_x=norm_x[0],
             wx_q=wx_q[0], wx_o=wx_o[0])
    w_in_p = _permute_cols(w_in[0], lay).astype(BF16)
    prm = _rwkv_params(lay, rw_mu[0], rw_w0[0], rw_w2[0], rw_a0[0], rw_a2[0], rw_g2[0],
                       rw_k_k[0], rw_k_a[0], rw_r_k[0], rw_ln_w[0], rw_ln_b[0])
    w_down_b = w_down[0].astype(BF16)

    xp = x_prompt.reshape(bp * t, d)
    proj_p = _matmul(_rmsnorm(xp, norm_mix[0], BF16), w_in_p, tn=512)
    proj_p3 = proj_p.reshape(bp, t, lay["np"])
    zeros_prev = [jnp.zeros((bp, 1, wd), F32) for wd in (rw, rw, rw, lay["lp"])]
    streams = _rwkv_prep(proj_p3, zeros_prev, lay, prm)
    o_a_p, sw_p = _rwkv_scan(streams, jnp.zeros((bp, nh_rw, RW_HEAD, RW_HEAD), F32), prm,
                             min(SCAN_CHUNK, t))
    k_rot_p, v_p, o_b_p = _moba_prompt(proj_p3, lay)
    x1_p = _mix_and_project(xp, proj_p, o_a_p.reshape(bp * t, rw), o_b_p.reshape(bp * t, mbw), w, lay)
    mem_n = _rmsnorm(mem_prompt.reshape(bp * n_mem, d), norm_mem[0], BF16)
    mk_p = _matmul(mem_n, wx_k[0])
    mv_p = _matmul(mem_n, wx_v[0])
    x2_p = _after_mix(x1_p, mk_p.reshape(bp, n_mem, xw), mv_p.reshape(bp, n_mem, xw), bp, t, w)
    hid_p, cv_p = _ffn_in_seq(_rmsnorm(x2_p, norm_ffn[0], BF16).reshape(bp, t, d), w_gate[0], w_up[0],
                              conv_w[0], conv_b[0], jnp.zeros((bp, CONV_W - 1, f), F32))
    x3_p = _matmul(hid_p.reshape(bp * t, f), w_down_b, res=x2_p,
                   tm=_pick(bp * t, (512, 256, 128, 64, 32, 16, 8)))
    y_p = _rmsnorm(x3_p, norm_out, F32).reshape(bp, t, d)

    xs = x_sample.reshape(db, d)
    proj_s = _matmul(_rmsnorm(xs, norm_mix[0], BF16), w_in_p)
    streams_s = _rwkv_prep(proj_s.reshape(db, 1, lay["np"]), _split_prev(state_rwkv_shift[0], lay),
                           lay, prm)
    streams_s = [jnp.pad(s, ((0, 0), (0, STEP_CHUNK - 1), (0, 0))) for s in streams_s]
    o_a_s, sw_s = _rwkv_scan(streams_s, state_rwkv_wkv[0], prm, STEP_CHUNK)
    o_a_s = o_a_s[:, 0, :]
    q_s, k_rot_s = _rope_rows(proj_s, lay, past_len)
    v_s = proj_s[:, lay["off_mv"]:lay["off_mv"] + mbw]
    heads3 = lambda z: z.reshape(db, nh_mb, MB_HEAD)
    block_sums = _block_sums(cache_moba_k[0], page_table)
    sel = _decode_select(heads3(q_s), block_sums)
    sel = sel.reshape(db, MB_TOPK, nh_mb).transpose(0, 2, 1)
    o_b_s = _decode_attend(heads3(q_s), heads3(k_rot_s), heads3(v_s), sel, page_table,
                           cache_moba_k[0], cache_moba_v[0])
    x1_s = _mix_and_project(xs, proj_s, o_a_s, o_b_s.astype(BF16), w, lay)
    x2_s = _after_mix(x1_s, cache_mem_k[0].reshape(db, n_mem, xw), cache_mem_v[0].reshape(db, n_mem, xw),
                      db, 1, w)
    conv_prev_s = state_ffn_conv[0]
    hid_s, gt_s = _ffn_in_step(_rmsnorm(x2_s, norm_ffn[0], BF16), w_gate[0], w_up[0], conv_w[0],
                               conv_b[0], conv_prev_s[:, 0], conv_prev_s[:, 1])
    x3_s = _matmul(hid_s, w_down_b, res=x2_s)
    y_s = _rmsnorm(x3_s, norm_out, F32).reshape(db, 1, d)
    cv_s = jnp.stack([conv_prev_s[:, 1], gt_s], axis=1)

    return (y_p, y_s, k_rot_p[None], v_p[None],
            k_rot_s.reshape(1, db, 1, nh_mb, MB_HEAD), v_s.reshape(1, db, 1, nh_mb, MB_HEAD),
            mk_p.reshape(1, bp, n_mem, xw // X_HEAD, X_HEAD), mv_p.reshape(1, bp, n_mem, xw // X_HEAD, X_HEAD),
            sw_p[None], sw_s[None],
            _unpermute_last(proj_p3[:, t - 1], lay)[None], _unpermute_last(proj_s, lay)[None],
            cv_p[None], cv_s[None])
```

```python
import functools
import math

import jax
import jax.numpy as jnp
from jax import lax
from jax.experimental import pallas as pl
from jax.experimental.pallas import tpu as pltpu

F32 = jnp.float32
BF16 = jnp.bfloat16

NORM_EPS = 1e-5
RW_HEAD = 64
RW_GN_EPS = 64e-5
MB_HEAD = 128
MB_BLOCK = 256
MB_TOPK = 3
PAGE_SIZE = 128
ROPE_DIM = MB_HEAD // 4
ROPE_THETA = 500000.0
X_HEAD = 128
CONV_W = 3

LANE = 128
VMEM_LIMIT = 56 * 1024 * 1024
SCAN_CHUNK = 64
SCAN_HEADS = 16
STEP_CHUNK = 8


def _params(sem):
    return pltpu.CompilerParams(dimension_semantics=sem, vmem_limit_bytes=VMEM_LIMIT)


def _pick(n, prefs):
    for p in prefs:
        if n % p == 0:
            return p
    return n


def _dot(a, b, dims=(((1,), (0,)), ((), ())), exact=False):
    if exact:
        return lax.dot_general(a, b, dims, precision=lax.Precision.HIGHEST,
                               preferred_element_type=F32)
    return lax.dot_general(a.astype(BF16), b.astype(BF16), dims, preferred_element_type=F32)


_NT = (((1,), (1,)), ((), ()))
_TN = (((0,), (0,)), ((), ()))


def _sigmoid(x):
    return 1.0 / (1.0 + jnp.exp(-x))


def _rmsnorm_kernel(x_ref, g_ref, o_ref):
    x = x_ref[...]
    y = x * lax.rsqrt(jnp.mean(x * x, axis=-1, keepdims=True) + NORM_EPS)
    o_ref[...] = (y * g_ref[...]).astype(o_ref.dtype)


def _rmsnorm(x, g, out_dtype):
    m, d = x.shape
    tr = _pick(m, (256, 128, 64, 32, 16, 8))
    return pl.pallas_call(
        _rmsnorm_kernel,
        grid=(m // tr,),
        in_specs=[pl.BlockSpec((tr, d), lambda i: (i, 0)),
                  pl.BlockSpec((1, d), lambda i: (0, 0))],
        out_specs=pl.BlockSpec((tr, d), lambda i: (i, 0)),
        out_shape=jax.ShapeDtypeStruct((m, d), out_dtype),
        compiler_params=_params(("parallel",)),
        name="rmsnorm",
    )(x, g.reshape(1, d))


def _mm_kernel(a_ref, w_ref, *refs):
    o_ref = refs[-1]
    out = _dot(a_ref[...], w_ref[...])
    if len(refs) == 2:
        out = out + refs[0][...]
    o_ref[...] = out.astype(o_ref.dtype)


def _matmul(a, w, res=None, out_dtype=F32, tm=None, tn=256):
    m, kdim = a.shape
    n = w.shape[1]
    tm = tm or _pick(m, (1024, 512, 256, 128, 64, 32, 16, 8))
    tn = min(tn, n)
    assert m % tm == 0
    in_specs = [pl.BlockSpec((tm, kdim), lambda i, j: (i, 0)),
                pl.BlockSpec((kdim, tn), lambda i, j: (0, j))]
    args = [a, w]
    if res is not None:
        in_specs.append(pl.BlockSpec((tm, tn), lambda i, j: (i, j)))
        args.append(res)
    return pl.pallas_call(
        _mm_kernel,
        grid=(m // tm, pl.cdiv(n, tn)),
        in_specs=in_specs,
        out_specs=pl.BlockSpec((tm, tn), lambda i, j: (i, j)),
        out_shape=jax.ShapeDtypeStruct((m, n), out_dtype),
        compiler_params=_params(("parallel", "parallel")),
        name="matmul",
    )(*args)


def _proj_kernel(a_ref, w_ref, o_ref, wb, carry, *, shift, first_shifted):
    j, i = pl.program_id(0), pl.program_id(1)

    @pl.when(i == 0)
    def _():
        wb[...] = w_ref[...].astype(BF16)

    r = jnp.dot(a_ref[...], wb[...], preferred_element_type=F32)
    if shift == 0:
        o_ref[...] = r
        return
    rolled = pltpu.roll(r, shift, 1)

    @pl.when(j < first_shifted)
    def _():
        o_ref[...] = r

    @pl.when(j >= first_shifted)
    def _():
        lane = lax.broadcasted_iota(jnp.int32, (r.shape[0], LANE), 1)
        head = jnp.where(lane < shift, carry[i], rolled[:, :LANE])
        o_ref[...] = jnp.concatenate([head, rolled[:, LANE:]], axis=1)

    carry[i] = rolled[:, :LANE]


def _project(xn, w_in, lay):
    m, kdim = xn.shape
    n_src = w_in.shape[1]
    tn, shift = lay["tn"], lay["shift"]
    tm = _pick(m, (1024, 512, 256, 128, 64, 32, 16, 8))
    n_j = pl.cdiv(n_src, tn)
    assert n_j * tn >= lay["np"] and lay["np"] == n_src + shift
    return pl.pallas_call(
        functools.partial(_proj_kernel, shift=shift, first_shifted=lay["off_q"] // tn),
        grid=(n_j, m // tm),
        in_specs=[pl.BlockSpec((tm, kdim), lambda j, i: (i, 0)),
                  pl.BlockSpec((kdim, tn), lambda j, i: (0, j))],
        out_specs=pl.BlockSpec((tm, tn), lambda j, i: (i, j)),
        out_shape=jax.ShapeDtypeStruct((m, lay["np"]), F32),
        scratch_shapes=[pltpu.VMEM((kdim, tn), BF16), pltpu.VMEM((m // tm, tm, LANE), F32)],
        compiler_params=_params(("arbitrary", "arbitrary")),
        name="in_projection",
    )(xn, w_in)


def _prep_kernel(zr_ref, zk_ref, zv_ref, zl_ref, pr_ref, pk_ref, pv_ref, pl_ref,
                 mur_ref, muk_ref, muv_ref, mul_ref, w0_ref, a0_ref, kk_ref, ka_ref,
                 w2_ref, a2_ref, g2_ref,
                 r_ref, lw_ref, k_ref, v_ref, kkr_ref, a_ref, g_ref,
                 cr, ck, cv, cl, *, tp):
    t = pl.program_id(1)

    @pl.when(t == 0)
    def _():
        cr[...] = pr_ref[...]
        ck[...] = pk_ref[...]
        cv[...] = pv_ref[...]
        cl[...] = pl_ref[...]

    def shift_mix(z_ref, carry, mu_ref):
        z = z_ref[...]
        if tp == 1:
            prev = carry[...]
        else:
            first = lax.broadcasted_iota(jnp.int32, z.shape, 0) == 0
            prev = jnp.where(first, carry[...], pltpu.roll(z, 1, 0))
        carry[...] = z[tp - 1:tp, :]
        return z + mu_ref[...] * (prev - z)

    r = shift_mix(zr_ref, cr, mur_ref)
    k = shift_mix(zk_ref, ck, muk_ref)
    v = shift_mix(zv_ref, cv, muv_ref)
    xl = shift_mix(zl_ref, cl, mul_ref)
    u = w0_ref[...] + _dot(jnp.tanh(xl), w2_ref[...])
    lw = -math.exp(-0.5) * _sigmoid(u)
    a = _sigmoid(a0_ref[...] + _dot(xl, a2_ref[...]))
    g = _dot(_sigmoid(xl), g2_ref[...])
    r_ref[...] = r
    lw_ref[...] = lw
    k_ref[...] = k * (1.0 + (a - 1.0) * ka_ref[...])
    v_ref[...] = v
    kkr_ref[...] = k * kk_ref[...]
    a_ref[...] = a
    g_ref[...] = g


def _rwkv_prep(proj, prev, lay, prm):
    b, t, _ = proj.shape
    rw, lp = lay["rw"], lay["lp"]
    tp = _pick(t, (128, 64, 32, 16, 8))
    assert lay["off_l"] % lp == 0
    lblk = lay["off_l"] // lp
    row = lambda j: pl.BlockSpec((None, tp, rw), lambda bi, ti, j=j: (bi, ti, j))
    prev_spec = lambda w: pl.BlockSpec((None, 1, w), lambda bi, ti: (bi, 0, 0))
    vec = lambda w: pl.BlockSpec((1, w), lambda bi, ti: (0, 0))
    mat = pl.BlockSpec((lp, rw), lambda bi, ti: (0, 0))
    out_spec = pl.BlockSpec((None, tp, rw), lambda bi, ti: (bi, ti, 0))
    out_sds = jax.ShapeDtypeStruct((b, t, rw), F32)
    return pl.pallas_call(
        functools.partial(_prep_kernel, tp=tp),
        grid=(b, t // tp),
        in_specs=[row(0), row(1), row(2),
                  pl.BlockSpec((None, tp, lp), lambda bi, ti: (bi, ti, lblk)),
                  prev_spec(rw), prev_spec(rw), prev_spec(rw), prev_spec(lp),
                  vec(rw), vec(rw), vec(rw), vec(lp), vec(rw), vec(rw), vec(rw), vec(rw),
                  mat, mat, mat],
        out_specs=[out_spec] * 7,
        out_shape=[out_sds] * 7,
        scratch_shapes=[pltpu.VMEM((1, rw), F32), pltpu.VMEM((1, rw), F32),
                        pltpu.VMEM((1, rw), F32), pltpu.VMEM((1, lp), F32)],
        compiler_params=_params(("parallel", "arbitrary")),
        name="rwkv_prep",
    )(proj, proj, proj, proj, *prev,
      prm["mu_r"], prm["mu_k"], prm["mu_v"], prm["mu_l"], prm["w0"], prm["a0"],
      prm["k_k"], prm["k_a"], prm["w2p"], prm["a2p"], prm["g2p"])


def _scan_kernel(*refs, chunk, heads, n_chunks, n_side):
    if n_side:
        refs = refs[1:]
    (r_ref, lw_ref, k_ref, v_ref, kkr_ref, a_ref, g_ref, rk_ref, lnw_ref, lnb_ref, s0_ref) = refs[:11]
    page_refs = refs[11:11 + n_side]
    o_ref, sf_ref = refs[11 + n_side:13 + n_side]
    if n_side:
        side_ref = refs[13 + n_side]
        for s in range(n_side):
            side_ref[s] = jnp.sum(page_refs[s][...], axis=0)
    s_scr = refs[-1]
    c = pl.program_id(2)

    @pl.when(c == 0)
    def _():
        s_scr[...] = s0_ref[...]

    n = chunk
    row = lax.broadcasted_iota(jnp.int32, (n, n), 0)
    col = lax.broadcasted_iota(jnp.int32, (n, n), 1)
    incl = row >= col
    strict = row > col
    eye = (row == col).astype(F32)

    lw = lw_ref[...]
    cum = _dot(incl.astype(F32), lw, exact=True)
    cum_prev = cum - lw
    total = cum[n - 1:n, :]
    e_cum = jnp.exp(cum)
    e_prev = jnp.exp(cum_prev)
    e_neg = jnp.exp(-cum)
    e_rest = jnp.exp(total - cum)
    e_total = jnp.exp(total)

    r_all, k_all, v_all = r_ref[...], k_ref[...], v_ref[...]
    kkr_all, a_all, g_all = kkr_ref[...], a_ref[...], g_ref[...]
    rk_all, lnw_all, lnb_all = rk_ref[...], lnw_ref[...], lnb_ref[...]

    hs = range(heads)
    sls = [slice(h * RW_HEAD, (h + 1) * RW_HEAD) for h in hs]
    r = [r_all[:, sl] for sl in sls]
    k = [k_all[:, sl] for sl in sls]
    v = [v_all[:, sl] for sl in sls]
    kk, bb, q2 = [], [], []
    for h, sl in enumerate(sls):
        kkr = kkr_all[:, sl]
        kk_h = kkr / jnp.maximum(jnp.sqrt(jnp.sum(kkr * kkr, axis=-1, keepdims=True)), 1e-12)
        kk.append(kk_h)
        bb.append(kk_h * a_all[:, sl])
        q2.append(jnp.concatenate([kk_h * e_prev[:, sl], r[h] * e_cum[:, sl]], axis=0))
    p_k = [_dot(q2[h], k[h] * e_neg[:, sls[h]], _NT) for h in hs]
    p_b = [_dot(q2[h], bb[h] * e_neg[:, sls[h]], _NT) for h in hs]
    s = [s_scr[h] for h in hs]
    qs = [_dot(q2[h], s[h], _NT) for h in hs]
    m_k = [jnp.where(strict, p[:n], 0.0) for p in p_k]
    a_k = [jnp.where(incl, p[n:], 0.0) for p in p_k]
    a_b = [jnp.where(incl, p[n:], 0.0) for p in p_b]
    npow = [jnp.where(strict, -p[:n], 0.0) for p in p_b]
    tinv = [eye + x for x in npow]
    for _ in range(int(math.log2(n)) - 1):
        npow = [_dot(x, x) for x in npow]
        tinv = [t + _dot(t, x) for t, x in zip(tinv, npow)]
    rhs = [qs[h][:n] + _dot(m_k[h], v[h]) for h in hs]
    u = [_dot(tinv[h], rhs[h]) for h in hs]
    y = [qs[h][n:] + _dot(a_k[h], v[h]) - _dot(a_b[h], u[h]) for h in hs]
    for h, sl in enumerate(sls):
        s_scr[h] = (s[h] * e_total[:, sl] + _dot(v[h], k[h] * e_rest[:, sl], _TN)
                    - _dot(u[h], bb[h] * e_rest[:, sl], _TN))
    outs = []
    for h, sl in enumerate(sls):
        mu = jnp.mean(y[h], axis=-1, keepdims=True)
        yc = y[h] - mu
        var = jnp.mean(yc * yc, axis=-1, keepdims=True)
        yn = yc * lax.rsqrt(var + RW_GN_EPS) * lnw_all[:, sl] + lnb_all[:, sl]
        bonus = jnp.sum(r[h] * k[h] * rk_all[:, sl], axis=-1, keepdims=True) * v[h]
        outs.append((yn + bonus) * g_all[:, sl])
    o_ref[...] = jnp.concatenate(outs, axis=1).astype(o_ref.dtype)

    @pl.when(c == n_chunks - 1)
    def _():
        sf_ref[...] = s_scr[...]


def _rwkv_scan(streams, s0, prm, chunk, paged_keys=None):
    b, t, rw = streams[0].shape
    nh = rw // RW_HEAD
    hg = _pick(nh, (SCAN_HEADS, 4, 2))
    w = hg * RW_HEAD
    assert t % chunk == 0 and chunk & (chunk - 1) == 0 and w % LANE == 0
    n_chunks, n_groups = t // chunk, nh // hg
    stream_spec = pl.BlockSpec((None, chunk, w), lambda bi, gi, ci, *_: (bi, ci, gi))
    vec_spec = pl.BlockSpec((1, w), lambda bi, gi, ci, *_: (0, gi))
    state_spec = pl.BlockSpec((None, hg, RW_HEAD, RW_HEAD), lambda bi, gi, ci, *_: (bi, gi, 0, 0))
    in_specs = [stream_spec] * 7 + [vec_spec] * 3 + [state_spec]
    out_specs = [stream_spec, state_spec]
    out_shape = [jax.ShapeDtypeStruct((b, t, rw), BF16),
                 jax.ShapeDtypeStruct((b, nh, RW_HEAD, RW_HEAD), F32)]
    args = [*streams, prm["r_k"], prm["ln_w"], prm["ln_b"], s0]
    n_side, n_prefetch = 0, 0
    if paged_keys is not None:
        cache_k, page_ids = paged_keys
        _, page, nhk, d = cache_k.shape
        n_ids = page_ids.shape[0]
        n_steps = b * n_groups * n_chunks
        n_side = pl.cdiv(n_ids, n_steps)
        step = lambda bi, gi, ci: (bi * n_groups + gi) * n_chunks + ci

        def page_spec(s):
            return pl.BlockSpec(
                (None, page, nhk, d),
                lambda bi, gi, ci, ids: (ids[jnp.minimum(step(bi, gi, ci) * n_side + s, n_ids - 1)], 0, 0, 0))

        in_specs += [page_spec(s) for s in range(n_side)]
        out_specs.append(pl.BlockSpec((n_side, nhk, d), lambda bi, gi, ci, ids: (step(bi, gi, ci), 0, 0)))
        out_shape.append(jax.ShapeDtypeStruct((n_steps * n_side, nhk, d), F32))
        args = [page_ids] + args + [cache_k] * n_side
        n_prefetch = 1
    outs = pl.pallas_call(
        functools.partial(_scan_kernel, chunk=chunk, heads=hg, n_chunks=n_chunks, n_side=n_side),
        grid_spec=pltpu.PrefetchScalarGridSpec(
            num_scalar_prefetch=n_prefetch,
            grid=(b, n_groups, n_chunks),
            in_specs=in_specs,
            out_specs=out_specs,
            scratch_shapes=[pltpu.VMEM((hg, RW_HEAD, RW_HEAD), F32)],
        ),
        out_shape=out_shape,
        compiler_params=_params(("parallel", "parallel", "arbitrary")),
        name="rwkv_scan",
    )(*args)
    if paged_keys is not None:
        return outs[0], outs[1], outs[2][:n_ids]
    return outs[0], outs[1]


def _rope_tables(pos):
    half = ROPE_DIM // 2
    inv = 1.0 / (ROPE_THETA ** (jnp.arange(0, ROPE_DIM, 2, dtype=F32) / ROPE_DIM))
    ang = pos.astype(F32)[:, None] * inv[None, :]
    cos, sin = jnp.cos(ang), jnp.sin(ang)
    t = pos.shape[0]
    rest = MB_HEAD - ROPE_DIM
    c = jnp.concatenate([cos, cos, jnp.ones((t, rest), F32)], axis=1)
    s = jnp.concatenate([-sin, sin, jnp.zeros((t, rest), F32)], axis=1)
    del half
    return c, s


def _rope(x, c, s):
    half = ROPE_DIM // 2
    lane = lax.broadcasted_iota(jnp.int32, x.shape, 1)
    swapped = jnp.where(lane < half, pltpu.roll(x, MB_HEAD - half, 1), pltpu.roll(x, half, 1))
    return x * c + swapped * s


def _moba_prompt_kernel(q_ref, k_ref, v_ref, c_ref, s_ref, kout_hbm, vout_hbm, o_ref,
                        kbuf, vbuf, sems, *, nb):
    bi, hi = pl.program_id(0), pl.program_id(1)
    c, s = c_ref[...], s_ref[...]
    q = _rope(q_ref[...], c, s)
    k = _rope(k_ref[...], c, s)
    kbuf[...] = k
    vbuf[...] = v_ref[...]
    k_copy = pltpu.make_async_copy(kbuf, kout_hbm.at[bi, :, hi, :], sems.at[0])
    v_copy = pltpu.make_async_copy(vbuf, vout_hbm.at[bi, :, hi, :], sems.at[1])
    k_copy.start()
    v_copy.start()
    t = q.shape[0]
    k_means = jnp.sum(k.reshape(nb, MB_BLOCK, MB_HEAD), axis=1) * (1.0 / MB_BLOCK)
    nbp = -(-nb // 8) * 8
    if nbp > nb:
        k_means = jnp.concatenate([k_means, jnp.zeros((nbp - nb, MB_HEAD), F32)], axis=0)
    scores = _dot(k_means, q, _NT, exact=True)
    blk = lax.broadcasted_iota(jnp.int32, (nbp, t), 0)
    tpos = lax.broadcasted_iota(jnp.int32, (nbp, t), 1)
    rank = jnp.zeros((nbp, t), jnp.int32)
    for m in range(nb):
        sm = scores[m:m + 1, :]
        ahead = (sm > scores) | ((sm == scores) & (blk > m))
        rank = rank + jnp.where(ahead & (tpos >= (m + 1) * MB_BLOCK), 1, 0)
    sel = (rank < MB_TOPK) & (tpos >= (blk + 1) * MB_BLOCK)
    eye = (lax.broadcasted_iota(jnp.int32, (nbp, nbp), 0)
           == lax.broadcasted_iota(jnp.int32, (nbp, nbp), 1)).astype(F32)
    sel_rows = _dot(jnp.where(sel, 1.0, 0.0), eye, _TN)
    qb = q.astype(BF16)
    kb = k.astype(BF16)
    vb = v_ref[...].astype(BF16)
    scale = MB_HEAD ** -0.5
    rowi = lax.broadcasted_iota(jnp.int32, (MB_BLOCK, MB_BLOCK), 0)
    coli = lax.broadcasted_iota(jnp.int32, (MB_BLOCK, MB_BLOCK), 1)
    causal = rowi >= coli
    neg_inf = float("-inf")
    for i in range(nb):
        rows = slice(i * MB_BLOCK, (i + 1) * MB_BLOCK)
        q_i = qb[rows]
        logits = []
        for j in range(i + 1):
            cols = slice(j * MB_BLOCK, (j + 1) * MB_BLOCK)
            lg = _dot(q_i, kb[cols], _NT) * scale
            if j == i:
                mask = causal
            elif i <= MB_TOPK:
                mask = None
            else:
                mask = sel_rows[rows, j:j + 1] > 0.5
            if mask is not None:
                lg = jnp.where(mask, lg, neg_inf)
            logits.append(lg)
        mx = logits[0].max(axis=-1, keepdims=True)
        for lg in logits[1:]:
            mx = jnp.maximum(mx, lg.max(axis=-1, keepdims=True))
        den = jnp.zeros((MB_BLOCK, 1), F32)
        acc = jnp.zeros((MB_BLOCK, MB_HEAD), F32)
        for j, lg in enumerate(logits):
            pj = jnp.exp(lg - mx)
            den = den + jnp.sum(pj, axis=-1, keepdims=True)
            acc = acc + _dot(pj, vb[j * MB_BLOCK:(j + 1) * MB_BLOCK])
        o_ref[rows, :] = (acc / den).astype(o_ref.dtype)
    del t
    k_copy.wait()
    v_copy.wait()


def _moba_prompt(proj, lay):
    b, t, _ = proj.shape
    assert t % MB_BLOCK == 0
    mbw = lay["mb"]
    nh = mbw // MB_HEAD
    ctab, stab = _rope_tables(jnp.arange(t))
    qb, kb, vb = lay["off_q"] // MB_HEAD, lay["off_mk"] // MB_HEAD, lay["off_mv"] // MB_HEAD
    head = lambda off: pl.BlockSpec((None, t, MB_HEAD), lambda bi, hi, off=off: (bi, 0, off + hi))
    tab = pl.BlockSpec((t, MB_HEAD), lambda bi, hi: (0, 0))
    hbm = pl.BlockSpec(memory_space=pl.ANY)
    rows_sds = jax.ShapeDtypeStruct((b, t, nh, MB_HEAD), F32)
    return pl.pallas_call(
        functools.partial(_moba_prompt_kernel, nb=t // MB_BLOCK),
        grid=(b, nh),
        in_specs=[head(qb), head(kb), head(vb), tab, tab],
        out_specs=[hbm, hbm, pl.BlockSpec((None, t, MB_HEAD), lambda bi, hi: (bi, 0, hi))],
        out_shape=[rows_sds, rows_sds, jax.ShapeDtypeStruct((b, t, mbw), BF16)],
        scratch_shapes=[pltpu.VMEM((t, MB_HEAD), F32), pltpu.VMEM((t, MB_HEAD), F32),
                        pltpu.SemaphoreType.DMA((2,))],
        compiler_params=_params(("parallel", "parallel")),
        name="moba_prompt",
    )(proj, proj, proj, ctab, stab)


def _rope_rows_kernel(q_ref, k_ref, c_ref, s_ref, qo_ref, ko_ref, *, heads):
    c, s = c_ref[...], s_ref[...]
    for h in range(heads):
        sl = slice(h * MB_HEAD, (h + 1) * MB_HEAD)
        qo_ref[:, sl] = _rope(q_ref[:, sl], c, s)
        ko_ref[:, sl] = _rope(k_ref[:, sl], c, s)


def _rope_rows(proj2d, lay, pos):
    m = proj2d.shape[0]
    mbw = lay["mb"]
    ctab, stab = _rope_tables(jnp.full((1,), pos))
    cw = math.gcd(math.gcd(lay["off_q"], lay["off_mk"]), mbw)
    assert cw % MB_HEAD == 0
    spec = lambda off: pl.BlockSpec((m, cw), lambda i, off=off: (0, off // cw + i))
    tab = pl.BlockSpec((1, MB_HEAD), lambda i: (0, 0))
    out = pl.BlockSpec((m, cw), lambda i: (0, i))
    sds = jax.ShapeDtypeStruct((m, mbw), F32)
    return pl.pallas_call(
        functools.partial(_rope_rows_kernel, heads=cw // MB_HEAD),
        grid=(mbw // cw,),
        in_specs=[spec(lay["off_q"]), spec(lay["off_mk"]), tab, tab],
        out_specs=[out, out],
        out_shape=[sds, sds],
        compiler_params=_params(("arbitrary",)),
        name="rope_rows",
    )(proj2d, proj2d, ctab, stab)


def _decode_select_kernel(q_ref, ps_ref, idx_ref, *, nb, ppb):
    q = q_ref[...]
    page_sums = ps_ref[...]
    block_sums = jnp.sum(page_sums.reshape((nb, ppb) + page_sums.shape[1:]), axis=1)
    scores = jnp.sum(block_sums * (1.0 / MB_BLOCK) * q[None], axis=-1, keepdims=True)
    blk = lax.broadcasted_iota(jnp.int32, scores.shape, 0)
    rank = jnp.zeros(scores.shape, jnp.int32)
    for m in range(nb):
        sm = scores[m:m + 1]
        ahead = (sm > scores) | ((sm == scores) & (blk > m))
        rank = rank + ahead.astype(jnp.int32)
    for r in range(MB_TOPK):
        idx_ref[r] = jnp.sum(jnp.where(rank == r, blk, 0), axis=0)


def _decode_select(q, page_sums):
    db, nh, d = q.shape
    n_pages = page_sums.shape[1]
    ppb = MB_BLOCK // PAGE_SIZE
    nb = n_pages // ppb
    assert nb >= MB_TOPK
    return pl.pallas_call(
        functools.partial(_decode_select_kernel, nb=nb, ppb=ppb),
        grid=(db,),
        in_specs=[pl.BlockSpec((None, nh, d), lambda bi: (bi, 0, 0)),
                  pl.BlockSpec((None, n_pages, nh, d), lambda bi: (bi, 0, 0, 0))],
        out_specs=pl.BlockSpec((None, MB_TOPK, nh, 1), lambda bi: (bi, 0, 0, 0)),
        out_shape=jax.ShapeDtypeStruct((db, MB_TOPK, nh, 1), jnp.int32),
        compiler_params=_params(("parallel",)),
        name="moba_decode_select",
    )(q, page_sums)


def _decode_attend_kernel(sel_ref, pt_ref, q_ref, kn_ref, vn_ref, ck_hbm, cv_hbm, o_ref,
                          kbuf, vbuf, sems, *, nh, ppb):
    bi = pl.program_id(0)
    n_sel = MB_TOPK * ppb
    page = kbuf.shape[2]

    def head_copies(h):
        copies = []
        for r in range(MB_TOPK):
            blk = sel_ref[(bi * nh + h) * MB_TOPK + r]
            for j in range(ppb):
                phys = pt_ref[bi, blk * ppb + j]
                s = r * ppb + j
                copies.append(pltpu.make_async_copy(ck_hbm.at[phys, :, h, :], kbuf.at[h, s], sems.at[h]))
                copies.append(pltpu.make_async_copy(cv_hbm.at[phys, :, h, :], vbuf.at[h, s], sems.at[h]))
        return copies

    all_copies = [head_copies(h) for h in range(nh)]
    for copies in all_copies:
        for cp in copies:
            cp.start()

    scale = MB_HEAD ** -0.5
    for h in range(nh):
        for cp in all_copies[h]:
            cp.wait()
        q = q_ref[h:h + 1, :]
        keys = kbuf[h].reshape(n_sel * page, MB_HEAD)
        vals = vbuf[h].reshape(n_sel * page, MB_HEAD)
        lg = _dot(jnp.broadcast_to(q, (8, MB_HEAD)), keys, _NT)[0:1] * scale
        own = jnp.sum(q * kn_ref[h:h + 1, :], axis=-1, keepdims=True) * scale
        mx = jnp.maximum(own, lg.max(axis=-1, keepdims=True))
        p_own = jnp.exp(own - mx)
        p = jnp.exp(lg - mx)
        den = p_own + jnp.sum(p, axis=-1, keepdims=True)
        acc = p_own * vn_ref[h:h + 1, :] + _dot(jnp.broadcast_to(p, (8, p.shape[1])), vals)[0:1]
        o_ref[h:h + 1, :] = acc / den


def _decode_attend(q, k_new, v_new, sel, page_table, cache_k, cache_v):
    db, nh, d = q.shape
    page = cache_k.shape[1]
    ppb = MB_BLOCK // page
    n_sel = MB_TOPK * ppb
    vec = pl.BlockSpec((None, nh, d), lambda bi, sel_r, pt_r: (bi, 0, 0))
    hbm = pl.BlockSpec(memory_space=pl.ANY)
    out = pl.pallas_call(
        functools.partial(_decode_attend_kernel, nh=nh, ppb=ppb),
        grid_spec=pltpu.PrefetchScalarGridSpec(
            num_scalar_prefetch=2,
            grid=(db,),
            in_specs=[vec, vec, vec, hbm, hbm],
            out_specs=vec,
            scratch_shapes=[pltpu.VMEM((nh, n_sel, page, d), F32),
                            pltpu.VMEM((nh, n_sel, page, d), F32),
                            pltpu.SemaphoreType.DMA((nh,))],
        ),
        out_shape=jax.ShapeDtypeStruct((db, nh, d), F32),
        compiler_params=_params(("arbitrary",)),
        name="moba_decode_attend",
    )(sel.reshape(-1), page_table, q, k_new, v_new, cache_k, cache_v)
    return out.reshape(db, nh * d)


def _merge_kernel(oa_ref, ob_ref, wa_ref, wb_ref, ga_ref, gb_ref, o_ref):
    ya = _dot(oa_ref[...], wa_ref[...])
    yb = _dot(ob_ref[...], wb_ref[...])
    o_ref[...] = (_sigmoid(ga_ref[...]) * ya + _sigmoid(gb_ref[...]) * yb).astype(o_ref.dtype)


def _merge(o_a, o_b, w_a, w_b, proj2d, lay):
    m, ka = o_a.shape
    kb = o_b.shape[1]
    d = w_a.shape[1]
    tm = _pick(m, (1024, 512, 256, 128, 64, 32, 16, 8))
    tn = next(c for c in (512, 256, LANE)
              if d % c == 0 and lay["off_ga"] % c == 0 and lay["off_gb"] % c == 0)
    ga, gb = lay["off_ga"] // tn, lay["off_gb"] // tn
    return pl.pallas_call(
        _merge_kernel,
        grid=(m // tm, d // tn),
        in_specs=[pl.BlockSpec((tm, ka), lambda i, j: (i, 0)),
                  pl.BlockSpec((tm, kb), lambda i, j: (i, 0)),
                  pl.BlockSpec((ka, tn), lambda i, j: (0, j)),
                  pl.BlockSpec((kb, tn), lambda i, j: (0, j)),
                  pl.BlockSpec((tm, tn), lambda i, j: (i, ga + j)),
                  pl.BlockSpec((tm, tn), lambda i, j: (i, gb + j))],
        out_specs=pl.BlockSpec((tm, tn), lambda i, j: (i, j)),
        out_shape=jax.ShapeDtypeStruct((m, d), BF16),
        compiler_params=_params(("parallel", "parallel")),
        name="gated_merge",
    )(o_a, o_b, w_a, w_b, proj2d, proj2d)


def _xattn_kernel(q_ref, k_ref, v_ref, o_ref, *, heads):
    scale = X_HEAD ** -0.5
    for h in range(heads):
        sl = slice(h * X_HEAD, (h + 1) * X_HEAD)
        q = q_ref[:, sl]
        if q.shape[0] < 8:
            q = jnp.broadcast_to(q, (8, X_HEAD))
        lg = _dot(q, k_ref[:, sl], _NT) * scale
        p = jnp.exp(lg - lg.max(axis=-1, keepdims=True))
        o = _dot(p, v_ref[:, sl]) / jnp.sum(p, axis=-1, keepdims=True)
        o_ref[:, sl] = o[:o_ref.shape[0]].astype(o_ref.dtype)


def _xattn(q, mk, mv):
    b, t, xw = q.shape
    n_mem = mk.shape[1]
    tq = _pick(t, (512, 256, 128, 64, 32, 16, 8))
    qspec = pl.BlockSpec((None, tq, xw), lambda bi, ti: (bi, ti, 0))
    mspec = pl.BlockSpec((None, n_mem, xw), lambda bi, ti: (bi, 0, 0))
    return pl.pallas_call(
        functools.partial(_xattn_kernel, heads=xw // X_HEAD),
        grid=(b, t // tq),
        in_specs=[qspec, mspec, mspec],
        out_specs=qspec,
        out_shape=jax.ShapeDtypeStruct((b, t, xw), BF16),
        compiler_params=_params(("parallel", "parallel")),
        name="cross_attention",
    )(q, mk, mv)


def _ffn_in_seq_kernel(x_ref, wg_ref, wu_ref, cw_ref, cb_ref, prev_ref, h_ref, cn_ref, carry,
                       wg_b, wu_b, *, tm, n_t):
    t = pl.program_id(2)

    @pl.when((pl.program_id(1) == 0) & (t == 0))
    def _():
        wg_b[...] = wg_ref[...].astype(BF16)
        wu_b[...] = wu_ref[...].astype(BF16)

    @pl.when(t == 0)
    def _():
        carry[...] = prev_ref[...]

    x = x_ref[...]
    gt = _dot(x, wg_b[...])
    up = _dot(x, wu_b[...])
    rowi = lax.broadcasted_iota(jnp.int32, gt.shape, 0)
    c0, c1 = carry[0:1, :], carry[1:2, :]
    g1 = jnp.where(rowi == 0, c1, pltpu.roll(gt, 1, 0))
    g2 = jnp.where(rowi == 0, c0, jnp.where(rowi == 1, c1, pltpu.roll(gt, 2, 0)))
    acc = cb_ref[...] + g2 * cw_ref[0:1, :] + g1 * cw_ref[1:2, :] + gt * cw_ref[2:3, :]
    h_ref[...] = (acc * _sigmoid(acc) * up).astype(h_ref.dtype)
    carry[...] = gt[tm - 2:tm, :]

    @pl.when(t == n_t - 1)
    def _():
        cn_ref[...] = gt[tm - 2:tm, :]


def _ffn_in_seq(xn, w_gate, w_up, conv_w, conv_b, conv_prev):
    b, t, d = xn.shape
    f = w_gate.shape[1]
    tm = _pick(t, (1024, 512, 256, 128, 64, 32, 16, 8))
    tn = _pick(f, (256, 128))
    assert tm >= CONV_W - 1
    n_t = t // tm
    wspec = pl.BlockSpec((d, tn), lambda j, bi, ti: (0, j))
    return pl.pallas_call(
        functools.partial(_ffn_in_seq_kernel, tm=tm, n_t=n_t),
        grid=(f // tn, b, n_t),
        in_specs=[pl.BlockSpec((None, tm, d), lambda j, bi, ti: (bi, ti, 0)),
                  wspec, wspec,
                  pl.BlockSpec((CONV_W, tn), lambda j, bi, ti: (0, j)),
                  pl.BlockSpec((1, tn), lambda j, bi, ti: (0, j)),
                  pl.BlockSpec((None, CONV_W - 1, tn), lambda j, bi, ti: (bi, 0, j))],
        out_specs=[pl.BlockSpec((None, tm, tn), lambda j, bi, ti: (bi, ti, j)),
                   pl.BlockSpec((None, CONV_W - 1, tn), lambda j, bi, ti: (bi, 0, j))],
        out_shape=[jax.ShapeDtypeStruct((b, t, f), BF16),
                   jax.ShapeDtypeStruct((b, CONV_W - 1, f), F32)],
        scratch_shapes=[pltpu.VMEM((CONV_W - 1, tn), F32),
                        pltpu.VMEM((d, tn), BF16), pltpu.VMEM((d, tn), BF16)],
        compiler_params=_params(("parallel", "arbitrary", "arbitrary")),
        name="ffn_in_seq",
    )(xn, w_gate, w_up, conv_w, conv_b.reshape(1, f), conv_prev)


def _ffn_in_step_kernel(x_ref, wg_ref, wu_ref, cw_ref, cb_ref, p0_ref, p1_ref, h_ref, gt_ref):
    x = x_ref[...]
    gt = _dot(x, wg_ref[...])
    up = _dot(x, wu_ref[...])
    acc = (cb_ref[...] + p0_ref[...] * cw_ref[0:1, :] + p1_ref[...] * cw_ref[1:2, :]
           + gt * cw_ref[2:3, :])
    h_ref[...] = (acc * _sigmoid(acc) * up).astype(h_ref.dtype)
    gt_ref[...] = gt


def _ffn_in_step(xn, w_gate, w_up, conv_w, conv_b, prev0, prev1):
    m, d = xn.shape
    f = w_gate.shape[1]
    tn = _pick(f, (256, 128))
    wspec = pl.BlockSpec((d, tn), lambda j: (0, j))
    rows = pl.BlockSpec((m, tn), lambda j: (0, j))
    return pl.pallas_call(
        _ffn_in_step_kernel,
        grid=(f // tn,),
        in_specs=[pl.BlockSpec((m, d), lambda j: (0, 0)), wspec, wspec,
                  pl.BlockSpec((CONV_W, tn), lambda j: (0, j)),
                  pl.BlockSpec((1, tn), lambda j: (0, j)), rows, rows],
        out_specs=[rows, rows],
        out_shape=[jax.ShapeDtypeStruct((m, f), BF16), jax.ShapeDtypeStruct((m, f), F32)],
        compiler_params=_params(("parallel",)),
        name="ffn_in_step",
    )(xn, w_gate, w_up, conv_w, conv_b.reshape(1, f), prev0, prev1)


def _layout(d_model, rw_cols):
    rw = d_model // 2
    mb = d_model // 2
    lora = rw_cols - 3 * rw
    tn = next(c for c in (512, 256, LANE) if (3 * rw) % c == 0 and (-lora) % c <= LANE)
    shift = (-lora) % tn
    lp = lora + shift
    off_l = 3 * rw
    off_q = off_l + lp
    off_ga = off_q + 3 * mb
    return dict(rw=rw, mb=mb, lora=lora, lp=lp, tn=tn, shift=shift, off_l=off_l, off_q=off_q,
                off_mk=off_q + mb, off_mv=off_q + 2 * mb, off_ga=off_ga, off_gb=off_ga + d_model,
                np=off_ga + 2 * d_model)


def _split_prev(prev, lay):
    rw, lora, lp = lay["rw"], lay["lora"], lay["lp"]
    b = prev.shape[0]
    parts = [prev[:, i * rw:(i + 1) * rw] for i in range(3)]
    parts.append(jnp.pad(prev[:, 3 * rw:3 * rw + lora], ((0, 0), (0, lp - lora))))
    return [p.reshape(b, 1, -1) for p in parts]


def _unpermute_last(proj_last, lay):
    rw3, lora = 3 * lay["rw"], lay["lora"]
    return jnp.concatenate([proj_last[:, :rw3], proj_last[:, lay["off_l"]:lay["off_l"] + lora]], axis=-1)


def _rwkv_params(lay, rw_mu, rw_w0, rw_w2, rw_a0, rw_a2, rw_g2, rw_k_k, rw_k_a, rw_r_k, rw_ln_w, rw_ln_b):
    rw, lora, lp = lay["rw"], lay["lora"], lay["lp"]
    dw, da = rw_w2.shape[0], rw_a2.shape[0]
    row = lambda x: x.reshape(1, -1)

    def padded(w, start):
        return jnp.zeros((lp, rw), BF16).at[start:start + w.shape[0]].set(w.astype(BF16))

    return dict(
        mu_r=row(rw_mu[:rw]), mu_k=row(rw_mu[rw:2 * rw]), mu_v=row(rw_mu[2 * rw:3 * rw]),
        mu_l=row(jnp.pad(rw_mu[3 * rw:], (0, lp - lora))),
        w0=row(rw_w0), a0=row(rw_a0), k_k=row(rw_k_k), k_a=row(rw_k_a),
        w2p=padded(rw_w2, 0), a2p=padded(rw_a2, dw), g2p=padded(rw_g2, dw + da),
        r_k=row(rw_r_k), ln_w=row(rw_ln_w), ln_b=row(rw_ln_b))


def _mix_and_project(x2d, proj2d, o_a, o_b, w, lay):
    merged = _merge(o_a, o_b, w["w_a_out"], w["w_b_out"], proj2d, lay)
    return _matmul(merged, w["w_o"], res=x2d, tn=512)


def _after_mix(x1, mk, mv, b, t, w):
    xn = _rmsnorm(x1, w["norm_x"], BF16)
    q = _matmul(xn, w["wx_q"], out_dtype=BF16)
    o = _xattn(q.reshape(b, t, -1), mk, mv)
    return _matmul(o.reshape(b * t, -1), w["wx_o"], res=x1, tn=1024)


def kernel(x_prompt, x_sample, mem_prompt, cache_moba_k, cache_moba_v, page_table, cache_mem_k, cache_mem_v, state_rwkv_wkv, state_rwkv_shift, state_ffn_conv, norm_mix, w_in, rw_mu, rw_w0, rw_w2, rw_a0, rw_a2, rw_g2, rw_k_k, rw_k_a, rw_r_k, rw_ln_w, rw_ln_b, w_a_out, w_b_out, w_o, norm_x, norm_mem, wx_q, wx_k, wx_v, wx_o, norm_ffn, w_gate, w_up, conv_w, conv_b, w_down, norm_out):
    assert w_in.shape[0] == 1, "single-layer model"
    bp, t, d = x_prompt.shape
    db, ts, _ = x_sample.shape
    assert ts == 1, "decode group advances one token"
    rw_cols = state_rwkv_shift.shape[-1]
    lay = _layout(d, rw_cols)
    rw, mbw = lay["rw"], lay["mb"]
    nh_rw, nh_mb = rw // RW_HEAD, mbw // MB_HEAD
    f = w_gate.shape[-1]
    n_mem = mem_prompt.shape[1]
    xw = wx_q.shape[-1]
    n_pages = page_table.shape[1]
    page = cache_moba_k.shape[2]
    assert page == PAGE_SIZE and n_pages % (MB_BLOCK // page) == 0
    past_len = n_pages * page

    w = dict(w_a_out=w_a_out[0], w_b_out=w_b_out[0], w_o=w_o[0], norm_x=norm_x[0],
             wx_q=wx_q[0], wx_o=wx_o[0])
    prm = _rwkv_params(lay, rw_mu[0], rw_w0[0], rw_w2[0], rw_a0[0], rw_a2[0], rw_g2[0],
                       rw_k_k[0], rw_k_a[0], rw_r_k[0], rw_ln_w[0], rw_ln_b[0])
    w_down_b = w_down[0].astype(BF16)

    xp = x_prompt.reshape(bp * t, d)
    proj_p = _project(_rmsnorm(xp, norm_mix[0], BF16), w_in[0], lay)
    proj_p3 = proj_p.reshape(bp, t, lay["np"])
    zeros_prev = [jnp.zeros((bp, 1, wd), F32) for wd in (rw, rw, rw, lay["lp"])]
    streams = _rwkv_prep(proj_p3, zeros_prev, lay, prm)
    o_a_p, sw_p, page_sums = _rwkv_scan(
        streams, jnp.zeros((bp, nh_rw, RW_HEAD, RW_HEAD), F32), prm, min(SCAN_CHUNK, t),
        paged_keys=(cache_moba_k[0], page_table.reshape(-1)))
    k_rot_p, v_p, o_b_p = _moba_prompt(proj_p3, lay)
    x1_p = _mix_and_project(xp, proj_p, o_a_p.reshape(bp * t, rw), o_b_p.reshape(bp * t, mbw), w, lay)
    mem_n = _rmsnorm(mem_prompt.reshape(bp * n_mem, d), norm_mem[0], BF16)
    mk_p = _matmul(mem_n, wx_k[0])
    mv_p = _matmul(mem_n, wx_v[0])
    x2_p = _after_mix(x1_p, mk_p.reshape(bp, n_mem, xw), mv_p.reshape(bp, n_mem, xw), bp, t, w)
    hid_p, cv_p = _ffn_in_seq(_rmsnorm(x2_p, norm_ffn[0], BF16).reshape(bp, t, d), w_gate[0], w_up[0],
                              conv_w[0], conv_b[0], jnp.zeros((bp, CONV_W - 1, f), F32))
    x3_p = _matmul(hid_p.reshape(bp * t, f), w_down_b, res=x2_p,
                   tm=_pick(bp * t, (512, 256, 128, 64, 32, 16, 8)))
    y_p = _rmsnorm(x3_p, norm_out, F32).reshape(bp, t, d)

    xs = x_sample.reshape(db, d)
    proj_s = _project(_rmsnorm(xs, norm_mix[0], BF16), w_in[0], lay)
    streams_s = _rwkv_prep(proj_s.reshape(db, 1, lay["np"]), _split_prev(state_rwkv_shift[0], lay),
                           lay, prm)
    streams_s = [jnp.pad(s, ((0, 0), (0, STEP_CHUNK - 1), (0, 0))) for s in streams_s]
    o_a_s, sw_s = _rwkv_scan(streams_s, state_rwkv_wkv[0], prm, STEP_CHUNK)
    o_a_s = o_a_s[:, 0, :]
    q_s, k_rot_s = _rope_rows(proj_s, lay, past_len)
    v_s = proj_s[:, lay["off_mv"]:lay["off_mv"] + mbw]
    heads3 = lambda z: z.reshape(db, nh_mb, MB_HEAD)
    sel = _decode_select(heads3(q_s), page_sums.reshape(db, n_pages, nh_mb, MB_HEAD))
    sel = sel.reshape(db, MB_TOPK, nh_mb).transpose(0, 2, 1)
    o_b_s = _decode_attend(heads3(q_s), heads3(k_rot_s), heads3(v_s), sel, page_table,
                           cache_moba_k[0], cache_moba_v[0])
    x1_s = _mix_and_project(xs, proj_s, o_a_s, o_b_s.astype(BF16), w, lay)
    x2_s = _after_mix(x1_s, cache_mem_k[0].reshape(db, n_mem, xw), cache_mem_v[0].reshape(db, n_mem, xw),
                      db, 1, w)
    conv_prev_s = state_ffn_conv[0]
    hid_s, gt_s = _ffn_in_step(_rmsnorm(x2_s, norm_ffn[0], BF16), w_gate[0], w_up[0], conv_w[0],
                               conv_b[0], conv_prev_s[:, 0], conv_prev_s[:, 1])
    x3_s = _matmul(hid_s, w_down_b, res=x2_s)
    y_s = _rmsnorm(x3_s, norm_out, F32).reshape(db, 1, d)
    cv_s = jnp.stack([conv_prev_s[:, 1], gt_s], axis=1)

    return (y_p, y_s, k_rot_p[None], v_p[None],
            k_rot_s.reshape(1, db, 1, nh_mb, MB_HEAD), v_s.reshape(1, db, 1, nh_mb, MB_HEAD),
            mk_p.reshape(1, bp, n_mem, xw // X_HEAD, X_HEAD), mv_p.reshape(1, bp, n_mem, xw // X_HEAD, X_HEAD),
            sw_p[None], sw_s[None],
            _unpermute_last(proj_p3[:, t - 1], lay)[None], _unpermute_last(proj_s, lay)[None],
            cv_p[None], cv_s[None])
```

```python
import functools
import math

import jax
import jax.numpy as jnp
from jax import lax
from jax.experimental import pallas as pl
from jax.experimental.pallas import tpu as pltpu

F32 = jnp.float32
BF16 = jnp.bfloat16

NORM_EPS = 1e-5
RW_HEAD = 64
RW_GN_EPS = 64e-5
MB_HEAD = 128
MB_BLOCK = 256
MB_TOPK = 3
PAGE_SIZE = 128
ROPE_DIM = MB_HEAD // 4
ROPE_THETA = 500000.0
X_HEAD = 128
CONV_W = 3

LANE = 128
VMEM_LIMIT = 56 * 1024 * 1024
SCAN_CHUNK = 64
SCAN_HEADS = 16
STEP_CHUNK = 8


def _params(sem):
    return pltpu.CompilerParams(dimension_semantics=sem, vmem_limit_bytes=VMEM_LIMIT)


def _pick(n, prefs):
    for p in prefs:
        if n % p == 0:
            return p
    return n


def _dot(a, b, dims=(((1,), (0,)), ((), ())), exact=False):
    if exact:
        return lax.dot_general(a, b, dims, precision=lax.Precision.HIGHEST,
                               preferred_element_type=F32)
    return lax.dot_general(a.astype(BF16), b.astype(BF16), dims, preferred_element_type=F32)


_NT = (((1,), (1,)), ((), ()))
_TN = (((0,), (0,)), ((), ()))


def _sigmoid(x):
    return 1.0 / (1.0 + jnp.exp(-x))


def _rmsnorm_kernel(x_ref, g_ref, o_ref):
    x = x_ref[...]
    y = x * lax.rsqrt(jnp.mean(x * x, axis=-1, keepdims=True) + NORM_EPS)
    o_ref[...] = (y * g_ref[...]).astype(o_ref.dtype)


def _rmsnorm(x, g, out_dtype):
    m, d = x.shape
    tr = _pick(m, (256, 128, 64, 32, 16, 8))
    return pl.pallas_call(
        _rmsnorm_kernel,
        grid=(m // tr,),
        in_specs=[pl.BlockSpec((tr, d), lambda i: (i, 0)),
                  pl.BlockSpec((1, d), lambda i: (0, 0))],
        out_specs=pl.BlockSpec((tr, d), lambda i: (i, 0)),
        out_shape=jax.ShapeDtypeStruct((m, d), out_dtype),
        compiler_params=_params(("parallel",)),
        name="rmsnorm",
    )(x, g.reshape(1, d))


def _mm_kernel(a_ref, w_ref, *refs):
    o_ref = refs[-1]
    out = _dot(a_ref[...], w_ref[...])
    if len(refs) == 2:
        out = out + refs[0][...]
    o_ref[...] = out.astype(o_ref.dtype)


def _matmul(a, w, res=None, out_dtype=F32, tm=None, tn=256):
    m, kdim = a.shape
    n = w.shape[1]
    tm = tm or _pick(m, (1024, 512, 256, 128, 64, 32, 16, 8))
    tn = min(tn, n)
    assert m % tm == 0
    in_specs = [pl.BlockSpec((tm, kdim), lambda i, j: (i, 0)),
                pl.BlockSpec((kdim, tn), lambda i, j: (0, j))]
    args = [a, w]
    if res is not None:
        in_specs.append(pl.BlockSpec((tm, tn), lambda i, j: (i, j)))
        args.append(res)
    return pl.pallas_call(
        _mm_kernel,
        grid=(m // tm, pl.cdiv(n, tn)),
        in_specs=in_specs,
        out_specs=pl.BlockSpec((tm, tn), lambda i, j: (i, j)),
        out_shape=jax.ShapeDtypeStruct((m, n), out_dtype),
        compiler_params=_params(("parallel", "parallel")),
        name="matmul",
    )(*args)


def _proj_kernel(a_ref, wt_ref, o_ref, wb):
    @pl.when(pl.program_id(1) == 0)
    def _():
        wb[...] = wt_ref[...].astype(BF16)

    o_ref[...] = lax.dot_general(a_ref[...], wb[...], _NT, preferred_element_type=F32)


def _project(xn, w_in_t, lay):
    m, kdim = xn.shape
    n_src = w_in_t.shape[0]
    tn, shift = lay["tn"], lay["shift"]
    first_shifted = lay["off_q"] // tn
    tm = _pick(m, (1024, 512, 256, 128, 64, 32, 16, 8))
    sub = 8
    assert lay["np"] % tn == 0 and lay["np"] == n_src + shift and shift % sub == 0
    return pl.pallas_call(
        _proj_kernel,
        grid=(lay["np"] // tn, m // tm),
        in_specs=[pl.BlockSpec((tm, kdim), lambda j, i: (i, 0)),
                  pl.BlockSpec((pl.Element(tn), pl.Element(kdim)),
                               lambda j, i: ((j * (tn // sub)
                                              - jnp.where(j >= first_shifted, shift // sub, 0)) * sub, 0))],
        out_specs=pl.BlockSpec((tm, tn), lambda j, i: (i, j)),
        out_shape=jax.ShapeDtypeStruct((m, lay["np"]), F32),
        scratch_shapes=[pltpu.VMEM((tn, kdim), BF16)],
        compiler_params=_params(("parallel", "arbitrary")),
        name="in_projection",
    )(xn, w_in_t)


def _prep_kernel(zr_ref, zk_ref, zv_ref, zl_ref, pr_ref, pk_ref, pv_ref, pl_ref,
                 mur_ref, muk_ref, muv_ref, mul_ref, w0_ref, a0_ref, kk_ref, ka_ref,
                 w2_ref, a2_ref, g2_ref,
                 r_ref, lw_ref, k_ref, v_ref, kkr_ref, a_ref, g_ref,
                 cr, ck, cv, cl, *, tp):
    t = pl.program_id(1)

    @pl.when(t == 0)
    def _():
        cr[...] = pr_ref[...]
        ck[...] = pk_ref[...]
        cv[...] = pv_ref[...]
        cl[...] = pl_ref[...]

    def shift_mix(z_ref, carry, mu_ref):
        z = z_ref[...]
        if tp == 1:
            prev = carry[...]
        else:
            first = lax.broadcasted_iota(jnp.int32, z.shape, 0) == 0
            prev = jnp.where(first, carry[...], pltpu.roll(z, 1, 0))
        carry[...] = z[tp - 1:tp, :]
        return z + mu_ref[...] * (prev - z)

    r = shift_mix(zr_ref, cr, mur_ref)
    k = shift_mix(zk_ref, ck, muk_ref)
    v = shift_mix(zv_ref, cv, muv_ref)
    xl = shift_mix(zl_ref, cl, mul_ref)
    u = w0_ref[...] + _dot(jnp.tanh(xl), w2_ref[...])
    lw = -math.exp(-0.5) * _sigmoid(u)
    a = _sigmoid(a0_ref[...] + _dot(xl, a2_ref[...]))
    g = _dot(_sigmoid(xl), g2_ref[...])
    r_ref[...] = r
    lw_ref[...] = lw
    k_ref[...] = k * (1.0 + (a - 1.0) * ka_ref[...])
    v_ref[...] = v
    kkr_ref[...] = k * kk_ref[...]
    a_ref[...] = a
    g_ref[...] = g


def _rwkv_prep(proj, prev, lay, prm):
    b, t, _ = proj.shape
    rw, lp = lay["rw"], lay["lp"]
    tp = _pick(t, (128, 64, 32, 16, 8))
    assert lay["off_l"] % lp == 0
    lblk = lay["off_l"] // lp
    row = lambda j: pl.BlockSpec((None, tp, rw), lambda bi, ti, j=j: (bi, ti, j))
    prev_spec = lambda w: pl.BlockSpec((None, 1, w), lambda bi, ti: (bi, 0, 0))
    vec = lambda w: pl.BlockSpec((1, w), lambda bi, ti: (0, 0))
    mat = pl.BlockSpec((lp, rw), lambda bi, ti: (0, 0))
    out_spec = pl.BlockSpec((None, tp, rw), lambda bi, ti: (bi, ti, 0))
    out_sds = jax.ShapeDtypeStruct((b, t, rw), F32)
    return pl.pallas_call(
        functools.partial(_prep_kernel, tp=tp),
        grid=(b, t // tp),
        in_specs=[row(0), row(1), row(2),
                  pl.BlockSpec((None, tp, lp), lambda bi, ti: (bi, ti, lblk)),
                  prev_spec(rw), prev_spec(rw), prev_spec(rw), prev_spec(lp),
                  vec(rw), vec(rw), vec(rw), vec(lp), vec(rw), vec(rw), vec(rw), vec(rw),
                  mat, mat, mat],
        out_specs=[out_spec] * 7,
        out_shape=[out_sds] * 7,
        scratch_shapes=[pltpu.VMEM((1, rw), F32), pltpu.VMEM((1, rw), F32),
                        pltpu.VMEM((1, rw), F32), pltpu.VMEM((1, lp), F32)],
        compiler_params=_params(("parallel", "arbitrary")),
        name="rwkv_prep",
    )(proj, proj, proj, proj, *prev,
      prm["mu_r"], prm["mu_k"], prm["mu_v"], prm["mu_l"], prm["w0"], prm["a0"],
      prm["k_k"], prm["k_a"], prm["w2p"], prm["a2p"], prm["g2p"])


def _scan_kernel(*refs, chunk, heads, n_chunks, n_side):
    if n_side:
        refs = refs[1:]
    (r_ref, lw_ref, k_ref, v_ref, kkr_ref, a_ref, g_ref, rk_ref, lnw_ref, lnb_ref, s0_ref) = refs[:11]
    page_refs = refs[11:11 + n_side]
    o_ref, sf_ref = refs[11 + n_side:13 + n_side]
    if n_side:
        side_ref = refs[13 + n_side]
        for s in range(n_side):
            side_ref[s] = jnp.sum(page_refs[s][...], axis=0)
    s_scr = refs[-1]
    c = pl.program_id(2)

    @pl.when(c == 0)
    def _():
        s_scr[...] = s0_ref[...]

    n = chunk
    row = lax.broadcasted_iota(jnp.int32, (n, n), 0)
    col = lax.broadcasted_iota(jnp.int32, (n, n), 1)
    incl = row >= col
    strict = row > col
    eye = (row == col).astype(F32)

    lw = lw_ref[...]
    cum = _dot(incl.astype(F32), lw, exact=True)
    cum_prev = cum - lw
    total = cum[n - 1:n, :]
    e_cum = jnp.exp(cum)
    e_prev = jnp.exp(cum_prev)
    e_neg = jnp.exp(-cum)
    e_rest = jnp.exp(total - cum)
    e_total = jnp.exp(total)

    r_all, k_all, v_all = r_ref[...], k_ref[...], v_ref[...]
    kkr_all, a_all, g_all = kkr_ref[...], a_ref[...], g_ref[...]
    rk_all, lnw_all, lnb_all = rk_ref[...], lnw_ref[...], lnb_ref[...]

    hs = range(heads)
    sls = [slice(h * RW_HEAD, (h + 1) * RW_HEAD) for h in hs]
    r = [r_all[:, sl] for sl in sls]
    k = [k_all[:, sl] for sl in sls]
    v = [v_all[:, sl] for sl in sls]
    kk, bb, q2 = [], [], []
    for h, sl in enumerate(sls):
        kkr = kkr_all[:, sl]
        kk_h = kkr / jnp.maximum(jnp.sqrt(jnp.sum(kkr * kkr, axis=-1, keepdims=True)), 1e-12)
        kk.append(kk_h)
        bb.append(kk_h * a_all[:, sl])
        q2.append(jnp.concatenate([kk_h * e_prev[:, sl], r[h] * e_cum[:, sl]], axis=0))
    p_k = [_dot(q2[h], k[h] * e_neg[:, sls[h]], _NT) for h in hs]
    p_b = [_dot(q2[h], bb[h] * e_neg[:, sls[h]], _NT) for h in hs]
    s = [s_scr[h] for h in hs]
    qs = [_dot(q2[h], s[h], _NT) for h in hs]
    m_k = [jnp.where(strict, p[:n], 0.0) for p in p_k]
    a_k = [jnp.where(incl, p[n:], 0.0) for p in p_k]
    a_b = [jnp.where(incl, p[n:], 0.0) for p in p_b]
    npow = [jnp.where(strict, -p[:n], 0.0) for p in p_b]
    tinv = [eye + x for x in npow]
    for _ in range(int(math.log2(n)) - 1):
        npow = [_dot(x, x) for x in npow]
        tinv = [t + _dot(t, x) for t, x in zip(tinv, npow)]
    rhs = [qs[h][:n] + _dot(m_k[h], v[h]) for h in hs]
    u = [_dot(tinv[h], rhs[h]) for h in hs]
    y = [qs[h][n:] + _dot(a_k[h], v[h]) - _dot(a_b[h], u[h]) for h in hs]
    for h, sl in enumerate(sls):
        s_scr[h] = (s[h] * e_total[:, sl] + _dot(v[h], k[h] * e_rest[:, sl], _TN)
                    - _dot(u[h], bb[h] * e_rest[:, sl], _TN))
    outs = []
    for h, sl in enumerate(sls):
        mu = jnp.mean(y[h], axis=-1, keepdims=True)
        yc = y[h] - mu
        var = jnp.mean(yc * yc, axis=-1, keepdims=True)
        yn = yc * lax.rsqrt(var + RW_GN_EPS) * lnw_all[:, sl] + lnb_all[:, sl]
        bonus = jnp.sum(r[h] * k[h] * rk_all[:, sl], axis=-1, keepdims=True) * v[h]
        outs.append((yn + bonus) * g_all[:, sl])
    o_ref[...] = jnp.concatenate(outs, axis=1).astype(o_ref.dtype)

    @pl.when(c == n_chunks - 1)
    def _():
        sf_ref[...] = s_scr[...]


def _rwkv_scan(streams, s0, prm, chunk, paged_keys=None):
    b, t, rw = streams[0].shape
    nh = rw // RW_HEAD
    hg = _pick(nh, (SCAN_HEADS, 4, 2))
    w = hg * RW_HEAD
    assert t % chunk == 0 and chunk & (chunk - 1) == 0 and w % LANE == 0
    n_chunks, n_groups = t // chunk, nh // hg
    stream_spec = pl.BlockSpec((None, chunk, w), lambda bi, gi, ci, *_: (bi, ci, gi))
    vec_spec = pl.BlockSpec((1, w), lambda bi, gi, ci, *_: (0, gi))
    state_spec = pl.BlockSpec((None, hg, RW_HEAD, RW_HEAD), lambda bi, gi, ci, *_: (bi, gi, 0, 0))
    in_specs = [stream_spec] * 7 + [vec_spec] * 3 + [state_spec]
    out_specs = [stream_spec, state_spec]
    out_shape = [jax.ShapeDtypeStruct((b, t, rw), BF16),
                 jax.ShapeDtypeStruct((b, nh, RW_HEAD, RW_HEAD), F32)]
    args = [*streams, prm["r_k"], prm["ln_w"], prm["ln_b"], s0]
    n_side, n_prefetch = 0, 0
    if paged_keys is not None:
        cache_k, page_ids = paged_keys
        _, page, nhk, d = cache_k.shape
        n_ids = page_ids.shape[0]
        n_steps = b * n_groups * n_chunks
        n_side = pl.cdiv(n_ids, n_steps)
        step = lambda bi, gi, ci: (bi * n_groups + gi) * n_chunks + ci

        def page_spec(s):
            return pl.BlockSpec(
                (None, page, nhk, d),
                lambda bi, gi, ci, ids: (ids[jnp.minimum(step(bi, gi, ci) * n_side + s, n_ids - 1)], 0, 0, 0))

        in_specs += [page_spec(s) for s in range(n_side)]
        out_specs.append(pl.BlockSpec((n_side, nhk, d), lambda bi, gi, ci, ids: (step(bi, gi, ci), 0, 0)))
        out_shape.append(jax.ShapeDtypeStruct((n_steps * n_side, nhk, d), F32))
        args = [page_ids] + args + [cache_k] * n_side
        n_prefetch = 1
    outs = pl.pallas_call(
        functools.partial(_scan_kernel, chunk=chunk, heads=hg, n_chunks=n_chunks, n_side=n_side),
        grid_spec=pltpu.PrefetchScalarGridSpec(
            num_scalar_prefetch=n_prefetch,
            grid=(b, n_groups, n_chunks),
            in_specs=in_specs,
            out_specs=out_specs,
            scratch_shapes=[pltpu.VMEM((hg, RW_HEAD, RW_HEAD), F32)],
        ),
        out_shape=out_shape,
        compiler_params=_params(("parallel", "parallel", "arbitrary")),
        name="rwkv_scan",
    )(*args)
    if paged_keys is not None:
        return outs[0], outs[1], outs[2][:n_ids]
    return outs[0], outs[1]


def _rope_tables(pos):
    half = ROPE_DIM // 2
    inv = 1.0 / (ROPE_THETA ** (jnp.arange(0, ROPE_DIM, 2, dtype=F32) / ROPE_DIM))
    ang = pos.astype(F32)[:, None] * inv[None, :]
    cos, sin = jnp.cos(ang), jnp.sin(ang)
    t = pos.shape[0]
    rest = MB_HEAD - ROPE_DIM
    c = jnp.concatenate([cos, cos, jnp.ones((t, rest), F32)], axis=1)
    s = jnp.concatenate([-sin, sin, jnp.zeros((t, rest), F32)], axis=1)
    del half
    return c, s


def _rope(x, c, s):
    half = ROPE_DIM // 2
    lane = lax.broadcasted_iota(jnp.int32, x.shape, 1)
    swapped = jnp.where(lane < half, pltpu.roll(x, MB_HEAD - half, 1), pltpu.roll(x, half, 1))
    return x * c + swapped * s


def _moba_prompt_kernel(q_ref, k_ref, v_ref, c_ref, s_ref, kout_hbm, vout_hbm, o_ref,
                        kbuf, vbuf, sems, *, nb):
    bi, hi = pl.program_id(0), pl.program_id(1)
    c, s = c_ref[...], s_ref[...]
    q = _rope(q_ref[...], c, s)
    k = _rope(k_ref[...], c, s)
    kbuf[...] = k
    vbuf[...] = v_ref[...]
    k_copy = pltpu.make_async_copy(kbuf, kout_hbm.at[bi, :, hi, :], sems.at[0])
    v_copy = pltpu.make_async_copy(vbuf, vout_hbm.at[bi, :, hi, :], sems.at[1])
    k_copy.start()
    v_copy.start()
    t = q.shape[0]
    k_means = jnp.sum(k.reshape(nb, MB_BLOCK, MB_HEAD), axis=1) * (1.0 / MB_BLOCK)
    nbp = -(-nb // 8) * 8
    if nbp > nb:
        k_means = jnp.concatenate([k_means, jnp.zeros((nbp - nb, MB_HEAD), F32)], axis=0)
    scores = _dot(k_means, q, _NT, exact=True)
    blk = lax.broadcasted_iota(jnp.int32, (nbp, t), 0)
    tpos = lax.broadcasted_iota(jnp.int32, (nbp, t), 1)
    rank = jnp.zeros((nbp, t), jnp.int32)
    for m in range(nb):
        sm = scores[m:m + 1, :]
        ahead = (sm > scores) | ((sm == scores) & (blk > m))
        rank = rank + jnp.where(ahead & (tpos >= (m + 1) * MB_BLOCK), 1, 0)
    sel = (rank < MB_TOPK) & (tpos >= (blk + 1) * MB_BLOCK)
    eye = (lax.broadcasted_iota(jnp.int32, (nbp, nbp), 0)
           == lax.broadcasted_iota(jnp.int32, (nbp, nbp), 1)).astype(F32)
    sel_rows = _dot(jnp.where(sel, 1.0, 0.0), eye, _TN)
    qb = q.astype(BF16)
    kb = k.astype(BF16)
    vb = v_ref[...].astype(BF16)
    scale = MB_HEAD ** -0.5
    rowi = lax.broadcasted_iota(jnp.int32, (MB_BLOCK, MB_BLOCK), 0)
    coli = lax.broadcasted_iota(jnp.int32, (MB_BLOCK, MB_BLOCK), 1)
    causal = rowi >= coli
    neg_inf = float("-inf")
    for i in range(nb):
        rows = slice(i * MB_BLOCK, (i + 1) * MB_BLOCK)
        q_i = qb[rows]
        logits = []
        for j in range(i + 1):
            cols = slice(j * MB_BLOCK, (j + 1) * MB_BLOCK)
            lg = _dot(q_i, kb[cols], _NT) * scale
            if j == i:
                mask = causal
            elif i <= MB_TOPK:
                mask = None
            else:
                mask = sel_rows[rows, j:j + 1] > 0.5
            if mask is not None:
                lg = jnp.where(mask, lg, neg_inf)
            logits.append(lg)
        mx = logits[0].max(axis=-1, keepdims=True)
        for lg in logits[1:]:
            mx = jnp.maximum(mx, lg.max(axis=-1, keepdims=True))
        den = jnp.zeros((MB_BLOCK, 1), F32)
        acc = jnp.zeros((MB_BLOCK, MB_HEAD), F32)
        for j, lg in enumerate(logits):
            pj = jnp.exp(lg - mx)
            den = den + jnp.sum(pj, axis=-1, keepdims=True)
            acc = acc + _dot(pj, vb[j * MB_BLOCK:(j + 1) * MB_BLOCK])
        o_ref[rows, :] = (acc / den).astype(o_ref.dtype)
    del t
    k_copy.wait()
    v_copy.wait()


def _moba_prompt(proj, lay):
    b, t, _ = proj.shape
    assert t % MB_BLOCK == 0
    mbw = lay["mb"]
    nh = mbw // MB_HEAD
    ctab, stab = _rope_tables(jnp.arange(t))
    qb, kb, vb = lay["off_q"] // MB_HEAD, lay["off_mk"] // MB_HEAD, lay["off_mv"] // MB_HEAD
    head = lambda off: pl.BlockSpec((None, t, MB_HEAD), lambda bi, hi, off=off: (bi, 0, off + hi))
    tab = pl.BlockSpec((t, MB_HEAD), lambda bi, hi: (0, 0))
    hbm = pl.BlockSpec(memory_space=pl.ANY)
    rows_sds = jax.ShapeDtypeStruct((b, t, nh, MB_HEAD), F32)
    return pl.pallas_call(
        functools.partial(_moba_prompt_kernel, nb=t // MB_BLOCK),
        grid=(b, nh),
        in_specs=[head(qb), head(kb), head(vb), tab, tab],
        out_specs=[hbm, hbm, pl.BlockSpec((None, t, MB_HEAD), lambda bi, hi: (bi, 0, hi))],
        out_shape=[rows_sds, rows_sds, jax.ShapeDtypeStruct((b, t, mbw), BF16)],
        scratch_shapes=[pltpu.VMEM((t, MB_HEAD), F32), pltpu.VMEM((t, MB_HEAD), F32),
                        pltpu.SemaphoreType.DMA((2,))],
        compiler_params=_params(("parallel", "parallel")),
        name="moba_prompt",
    )(proj, proj, proj, ctab, stab)


def _rope_rows_kernel(q_ref, k_ref, c_ref, s_ref, qo_ref, ko_ref, *, heads):
    c, s = c_ref[...], s_ref[...]
    for h in range(heads):
        sl = slice(h * MB_HEAD, (h + 1) * MB_HEAD)
        qo_ref[:, sl] = _rope(q_ref[:, sl], c, s)
        ko_ref[:, sl] = _rope(k_ref[:, sl], c, s)


def _rope_rows(proj2d, lay, pos):
    m = proj2d.shape[0]
    mbw = lay["mb"]
    ctab, stab = _rope_tables(jnp.full((1,), pos))
    cw = math.gcd(math.gcd(lay["off_q"], lay["off_mk"]), mbw)
    assert cw % MB_HEAD == 0
    spec = lambda off: pl.BlockSpec((m, cw), lambda i, off=off: (0, off // cw + i))
    tab = pl.BlockSpec((1, MB_HEAD), lambda i: (0, 0))
    out = pl.BlockSpec((m, cw), lambda i: (0, i))
    sds = jax.ShapeDtypeStruct((m, mbw), F32)
    return pl.pallas_call(
        functools.partial(_rope_rows_kernel, heads=cw // MB_HEAD),
        grid=(mbw // cw,),
        in_specs=[spec(lay["off_q"]), spec(lay["off_mk"]), tab, tab],
        out_specs=[out, out],
        out_shape=[sds, sds],
        compiler_params=_params(("arbitrary",)),
        name="rope_rows",
    )(proj2d, proj2d, ctab, stab)


def _decode_select_kernel(q_ref, ps_ref, idx_ref, *, nb, ppb):
    q = q_ref[...]
    page_sums = ps_ref[...]
    block_sums = jnp.sum(page_sums.reshape((nb, ppb) + page_sums.shape[1:]), axis=1)
    scores = jnp.sum(block_sums * (1.0 / MB_BLOCK) * q[None], axis=-1, keepdims=True)
    blk = lax.broadcasted_iota(jnp.int32, scores.shape, 0)
    rank = jnp.zeros(scores.shape, jnp.int32)
    for m in range(nb):
        sm = scores[m:m + 1]
        ahead = (sm > scores) | ((sm == scores) & (blk > m))
        rank = rank + ahead.astype(jnp.int32)
    for r in range(MB_TOPK):
        idx_ref[r] = jnp.sum(jnp.where(rank == r, blk, 0), axis=0)


def _decode_select(q, page_sums):
    db, nh, d = q.shape
    n_pages = page_sums.shape[1]
    ppb = MB_BLOCK // PAGE_SIZE
    nb = n_pages // ppb
    assert nb >= MB_TOPK
    return pl.pallas_call(
        functools.partial(_decode_select_kernel, nb=nb, ppb=ppb),
        grid=(db,),
        in_specs=[pl.BlockSpec((None, nh, d), lambda bi: (bi, 0, 0)),
                  pl.BlockSpec((None, n_pages, nh, d), lambda bi: (bi, 0, 0, 0))],
        out_specs=pl.BlockSpec((None, MB_TOPK, nh, 1), lambda bi: (bi, 0, 0, 0)),
        out_shape=jax.ShapeDtypeStruct((db, MB_TOPK, nh, 1), jnp.int32),
        compiler_params=_params(("parallel",)),
        name="moba_decode_select",
    )(q, page_sums)


def _decode_attend_kernel(sel_ref, pt_ref, q_ref, kn_ref, vn_ref, ck_hbm, cv_hbm, o_ref,
                          kbuf, vbuf, sems, *, nh, ppb):
    bi = pl.program_id(0)
    n_sel = MB_TOPK * ppb
    page = kbuf.shape[2]

    def head_copies(h):
        copies = []
        for r in range(MB_TOPK):
            blk = sel_ref[(bi * nh + h) * MB_TOPK + r]
            for j in range(ppb):
                phys = pt_ref[bi, blk * ppb + j]
                s = r * ppb + j
                copies.append(pltpu.make_async_copy(ck_hbm.at[phys, :, h, :], kbuf.at[h, s], sems.at[h]))
                copies.append(pltpu.make_async_copy(cv_hbm.at[phys, :, h, :], vbuf.at[h, s], sems.at[h]))
        return copies

    all_copies = [head_copies(h) for h in range(nh)]
    for copies in all_copies:
        for cp in copies:
            cp.start()

    scale = MB_HEAD ** -0.5
    for h in range(nh):
        for cp in all_copies[h]:
            cp.wait()
        q = q_ref[h:h + 1, :]
        keys = kbuf[h].reshape(n_sel * page, MB_HEAD)
        vals = vbuf[h].reshape(n_sel * page, MB_HEAD)
        lg = _dot(jnp.broadcast_to(q, (8, MB_HEAD)), keys, _NT)[0:1] * scale
        own = jnp.sum(q * kn_ref[h:h + 1, :], axis=-1, keepdims=True) * scale
        mx = jnp.maximum(own, lg.max(axis=-1, keepdims=True))
        p_own = jnp.exp(own - mx)
        p = jnp.exp(lg - mx)
        den = p_own + jnp.sum(p, axis=-1, keepdims=True)
        acc = p_own * vn_ref[h:h + 1, :] + _dot(jnp.broadcast_to(p, (8, p.shape[1])), vals)[0:1]
        o_ref[h:h + 1, :] = acc / den


def _decode_attend(q, k_new, v_new, sel, page_table, cache_k, cache_v):
    db, nh, d = q.shape
    page = cache_k.shape[1]
    ppb = MB_BLOCK // page
    n_sel = MB_TOPK * ppb
    vec = pl.BlockSpec((None, nh, d), lambda bi, sel_r, pt_r: (bi, 0, 0))
    hbm = pl.BlockSpec(memory_space=pl.ANY)
    out = pl.pallas_call(
        functools.partial(_decode_attend_kernel, nh=nh, ppb=ppb),
        grid_spec=pltpu.PrefetchScalarGridSpec(
            num_scalar_prefetch=2,
            grid=(db,),
            in_specs=[vec, vec, vec, hbm, hbm],
            out_specs=vec,
            scratch_shapes=[pltpu.VMEM((nh, n_sel, page, d), F32),
                            pltpu.VMEM((nh, n_sel, page, d), F32),
                            pltpu.SemaphoreType.DMA((nh,))],
        ),
        out_shape=jax.ShapeDtypeStruct((db, nh, d), F32),
        compiler_params=_params(("arbitrary",)),
        name="moba_decode_attend",
    )(sel.reshape(-1), page_table, q, k_new, v_new, cache_k, cache_v)
    return out.reshape(db, nh * d)


def _merge_kernel(oa_ref, ob_ref, wa_ref, wb_ref, ga_ref, gb_ref, o_ref):
    ya = _dot(oa_ref[...], wa_ref[...])
    yb = _dot(ob_ref[...], wb_ref[...])
    o_ref[...] = (_sigmoid(ga_ref[...]) * ya + _sigmoid(gb_ref[...]) * yb).astype(o_ref.dtype)


def _merge(o_a, o_b, w_a, w_b, proj2d, lay):
    m, ka = o_a.shape
    kb = o_b.shape[1]
    d = w_a.shape[1]
    tm = _pick(m, (1024, 512, 256, 128, 64, 32, 16, 8))
    tn = next(c for c in (512, 256, LANE)
              if d % c == 0 and lay["off_ga"] % c == 0 and lay["off_gb"] % c == 0)
    ga, gb = lay["off_ga"] // tn, lay["off_gb"] // tn
    return pl.pallas_call(
        _merge_kernel,
        grid=(m // tm, d // tn),
        in_specs=[pl.BlockSpec((tm, ka), lambda i, j: (i, 0)),
                  pl.BlockSpec((tm, kb), lambda i, j: (i, 0)),
                  pl.BlockSpec((ka, tn), lambda i, j: (0, j)),
                  pl.BlockSpec((kb, tn), lambda i, j: (0, j)),
                  pl.BlockSpec((tm, tn), lambda i, j: (i, ga + j)),
                  pl.BlockSpec((tm, tn), lambda i, j: (i, gb + j))],
        out_specs=pl.BlockSpec((tm, tn), lambda i, j: (i, j)),
        out_shape=jax.ShapeDtypeStruct((m, d), BF16),
        compiler_params=_params(("parallel", "parallel")),
        name="gated_merge",
    )(o_a, o_b, w_a, w_b, proj2d, proj2d)


def _xattn_kernel(q_ref, k_ref, v_ref, o_ref, *, heads):
    scale = X_HEAD ** -0.5
    for h in range(heads):
        sl = slice(h * X_HEAD, (h + 1) * X_HEAD)
        q = q_ref[:, sl]
        if q.shape[0] < 8:
            q = jnp.broadcast_to(q, (8, X_HEAD))
        lg = _dot(q, k_ref[:, sl], _NT) * scale
        p = jnp.exp(lg - lg.max(axis=-1, keepdims=True))
        o = _dot(p, v_ref[:, sl]) / jnp.sum(p, axis=-1, keepdims=True)
        o_ref[:, sl] = o[:o_ref.shape[0]].astype(o_ref.dtype)


def _xattn(q, mk, mv):
    b, t, xw = q.shape
    n_mem = mk.shape[1]
    tq = _pick(t, (512, 256, 128, 64, 32, 16, 8))
    qspec = pl.BlockSpec((None, tq, xw), lambda bi, ti: (bi, ti, 0))
    mspec = pl.BlockSpec((None, n_mem, xw), lambda bi, ti: (bi, 0, 0))
    return pl.pallas_call(
        functools.partial(_xattn_kernel, heads=xw // X_HEAD),
        grid=(b, t // tq),
        in_specs=[qspec, mspec, mspec],
        out_specs=qspec,
        out_shape=jax.ShapeDtypeStruct((b, t, xw), BF16),
        compiler_params=_params(("parallel", "parallel")),
        name="cross_attention",
    )(q, mk, mv)


def _ffn_in_seq_kernel(x_ref, wg_ref, wu_ref, cw_ref, cb_ref, prev_ref, h_ref, cn_ref, carry,
                       wg_b, wu_b, *, tm, n_t):
    t = pl.program_id(2)

    @pl.when((pl.program_id(1) == 0) & (t == 0))
    def _():
        wg_b[...] = wg_ref[...].astype(BF16)
        wu_b[...] = wu_ref[...].astype(BF16)

    @pl.when(t == 0)
    def _():
        carry[...] = prev_ref[...]

    x = x_ref[...]
    gt = _dot(x, wg_b[...])
    up = _dot(x, wu_b[...])
    rowi = lax.broadcasted_iota(jnp.int32, gt.shape, 0)
    c0, c1 = carry[0:1, :], carry[1:2, :]
    g1 = jnp.where(rowi == 0, c1, pltpu.roll(gt, 1, 0))
    g2 = jnp.where(rowi == 0, c0, jnp.where(rowi == 1, c1, pltpu.roll(gt, 2, 0)))
    acc = cb_ref[...] + g2 * cw_ref[0:1, :] + g1 * cw_ref[1:2, :] + gt * cw_ref[2:3, :]
    h_ref[...] = (acc * _sigmoid(acc) * up).astype(h_ref.dtype)
    carry[...] = gt[tm - 2:tm, :]

    @pl.when(t == n_t - 1)
    def _():
        cn_ref[...] = gt[tm - 2:tm, :]


def _ffn_in_seq(xn, w_gate, w_up, conv_w, conv_b, conv_prev):
    b, t, d = xn.shape
    f = w_gate.shape[1]
    tm = _pick(t, (1024, 512, 256, 128, 64, 32, 16, 8))
    tn = _pick(f, (256, 128))
    assert tm >= CONV_W - 1
    n_t = t // tm
    wspec = pl.BlockSpec((d, tn), lambda j, bi, ti: (0, j))
    return pl.pallas_call(
        functools.partial(_ffn_in_seq_kernel, tm=tm, n_t=n_t),
        grid=(f // tn, b, n_t),
        in_specs=[pl.BlockSpec((None, tm, d), lambda j, bi, ti: (bi, ti, 0)),
                  wspec, wspec,
                  pl.BlockSpec((CONV_W, tn), lambda j, bi, ti: (0, j)),
                  pl.BlockSpec((1, tn), lambda j, bi, ti: (0, j)),
                  pl.BlockSpec((None, CONV_W - 1, tn), lambda j, bi, ti: (bi, 0, j))],
        out_specs=[pl.BlockSpec((None, tm, tn), lambda j, bi, ti: (bi, ti, j)),
                   pl.BlockSpec((None, CONV_W - 1, tn), lambda j, bi, ti: (bi, 0, j))],
        out_shape=[jax.ShapeDtypeStruct((b, t, f), BF16),
                   jax.ShapeDtypeStruct((b, CONV_W - 1, f), F32)],
        scratch_shapes=[pltpu.VMEM((CONV_W - 1, tn), F32),
                        pltpu.VMEM((d, tn), BF16), pltpu.VMEM((d, tn), BF16)],
        compiler_params=_params(("parallel", "arbitrary", "arbitrary")),
        name="ffn_in_seq",
    )(xn, w_gate, w_up, conv_w, conv_b.reshape(1, f), conv_prev)


def _ffn_in_step_kernel(x_ref, wg_ref, wu_ref, cw_ref, cb_ref, p0_ref, p1_ref, h_ref, gt_ref):
    x = x_ref[...]
    gt = _dot(x, wg_ref[...])
    up = _dot(x, wu_ref[...])
    acc = (cb_ref[...] + p0_ref[...] * cw_ref[0:1, :] + p1_ref[...] * cw_ref[1:2, :]
           + gt * cw_ref[2:3, :])
    h_ref[...] = (acc * _sigmoid(acc) * up).astype(h_ref.dtype)
    gt_ref[...] = gt


def _ffn_in_step(xn, w_gate, w_up, conv_w, conv_b, prev0, prev1):
    m, d = xn.shape
    f = w_gate.shape[1]
    tn = _pick(f, (256, 128))
    wspec = pl.BlockSpec((d, tn), lambda j: (0, j))
    rows = pl.BlockSpec((m, tn), lambda j: (0, j))
    return pl.pallas_call(
        _ffn_in_step_kernel,
        grid=(f // tn,),
        in_specs=[pl.BlockSpec((m, d), lambda j: (0, 0)), wspec, wspec,
                  pl.BlockSpec((CONV_W, tn), lambda j: (0, j)),
                  pl.BlockSpec((1, tn), lambda j: (0, j)), rows, rows],
        out_specs=[rows, rows],
        out_shape=[jax.ShapeDtypeStruct((m, f), BF16), jax.ShapeDtypeStruct((m, f), F32)],
        compiler_params=_params(("parallel",)),
        name="ffn_in_step",
    )(xn, w_gate, w_up, conv_w, conv_b.reshape(1, f), prev0, prev1)


def _layout(d_model, rw_cols):
    rw = d_model // 2
    mb = d_model // 2
    lora = rw_cols - 3 * rw
    tn = next(c for c in (512, 256, LANE) if (3 * rw) % c == 0 and (-lora) % c <= LANE)
    shift = (-lora) % tn
    lp = lora + shift
    off_l = 3 * rw
    off_q = off_l + lp
    off_ga = off_q + 3 * mb
    return dict(rw=rw, mb=mb, lora=lora, lp=lp, tn=tn, shift=shift, off_l=off_l, off_q=off_q,
                off_mk=off_q + mb, off_mv=off_q + 2 * mb, off_ga=off_ga, off_gb=off_ga + d_model,
                np=off_ga + 2 * d_model)


def _split_prev(prev, lay):
    rw, lora, lp = lay["rw"], lay["lora"], lay["lp"]
    b = prev.shape[0]
    parts = [prev[:, i * rw:(i + 1) * rw] for i in range(3)]
    parts.append(jnp.pad(prev[:, 3 * rw:3 * rw + lora], ((0, 0), (0, lp - lora))))
    return [p.reshape(b, 1, -1) for p in parts]


def _unpermute_last(proj_last, lay):
    rw3, lora = 3 * lay["rw"], lay["lora"]
    return jnp.concatenate([proj_last[:, :rw3], proj_last[:, lay["off_l"]:lay["off_l"] + lora]], axis=-1)


def _rwkv_params(lay, rw_mu, rw_w0, rw_w2, rw_a0, rw_a2, rw_g2, rw_k_k, rw_k_a, rw_r_k, rw_ln_w, rw_ln_b):
    rw, lora, lp = lay["rw"], lay["lora"], lay["lp"]
    dw, da = rw_w2.shape[0], rw_a2.shape[0]
    row = lambda x: x.reshape(1, -1)

    def padded(w, start):
        return jnp.zeros((lp, rw), BF16).at[start:start + w.shape[0]].set(w.astype(BF16))

    return dict(
        mu_r=row(rw_mu[:rw]), mu_k=row(rw_mu[rw:2 * rw]), mu_v=row(rw_mu[2 * rw:3 * rw]),
        mu_l=row(jnp.pad(rw_mu[3 * rw:], (0, lp - lora))),
        w0=row(rw_w0), a0=row(rw_a0), k_k=row(rw_k_k), k_a=row(rw_k_a),
        w2p=padded(rw_w2, 0), a2p=padded(rw_a2, dw), g2p=padded(rw_g2, dw + da),
        r_k=row(rw_r_k), ln_w=row(rw_ln_w), ln_b=row(rw_ln_b))


def _mix_and_project(x2d, proj2d, o_a, o_b, w, lay):
    merged = _merge(o_a, o_b, w["w_a_out"], w["w_b_out"], proj2d, lay)
    return _matmul(merged, w["w_o"], res=x2d, tn=512)


def _after_mix(x1, mk, mv, b, t, w):
    xn = _rmsnorm(x1, w["norm_x"], BF16)
    q = _matmul(xn, w["wx_q"], out_dtype=BF16)
    o = _xattn(q.reshape(b, t, -1), mk, mv)
    return _matmul(o.reshape(b * t, -1), w["wx_o"], res=x1, tn=1024)


def kernel(x_prompt, x_sample, mem_prompt, cache_moba_k, cache_moba_v, page_table, cache_mem_k, cache_mem_v, state_rwkv_wkv, state_rwkv_shift, state_ffn_conv, norm_mix, w_in, rw_mu, rw_w0, rw_w2, rw_a0, rw_a2, rw_g2, rw_k_k, rw_k_a, rw_r_k, rw_ln_w, rw_ln_b, w_a_out, w_b_out, w_o, norm_x, norm_mem, wx_q, wx_k, wx_v, wx_o, norm_ffn, w_gate, w_up, conv_w, conv_b, w_down, norm_out):
    assert w_in.shape[0] == 1, "single-layer model"
    bp, t, d = x_prompt.shape
    db, ts, _ = x_sample.shape
    assert ts == 1, "decode group advances one token"
    rw_cols = state_rwkv_shift.shape[-1]
    lay = _layout(d, rw_cols)
    rw, mbw = lay["rw"], lay["mb"]
    nh_rw, nh_mb = rw // RW_HEAD, mbw // MB_HEAD
    f = w_gate.shape[-1]
    n_mem = mem_prompt.shape[1]
    xw = wx_q.shape[-1]
    n_pages = page_table.shape[1]
    page = cache_moba_k.shape[2]
    assert page == PAGE_SIZE and n_pages % (MB_BLOCK // page) == 0
    past_len = n_pages * page

    w = dict(w_a_out=w_a_out[0], w_b_out=w_b_out[0], w_o=w_o[0], norm_x=norm_x[0],
             wx_q=wx_q[0], wx_o=wx_o[0])
    prm = _rwkv_params(lay, rw_mu[0], rw_w0[0], rw_w2[0], rw_a0[0], rw_a2[0], rw_g2[0],
                       rw_k_k[0], rw_k_a[0], rw_r_k[0], rw_ln_w[0], rw_ln_b[0])
    w_down_b = w_down[0].astype(BF16)
    w_in_t = jnp.swapaxes(w_in[0], 0, 1)

    xp = x_prompt.reshape(bp * t, d)
    proj_p = _project(_rmsnorm(xp, norm_mix[0], BF16), w_in_t, lay)
    proj_p3 = proj_p.reshape(bp, t, lay["np"])
    zeros_prev = [jnp.zeros((bp, 1, wd), F32) for wd in (rw, rw, rw, lay["lp"])]
    streams = _rwkv_prep(proj_p3, zeros_prev, lay, prm)
    o_a_p, sw_p, page_sums = _rwkv_scan(
        streams, jnp.zeros((bp, nh_rw, RW_HEAD, RW_HEAD), F32), prm, min(SCAN_CHUNK, t),
        paged_keys=(cache_moba_k[0], page_table.reshape(-1)))
    k_rot_p, v_p, o_b_p = _moba_prompt(proj_p3, lay)
    x1_p = _mix_and_project(xp, proj_p, o_a_p.reshape(bp * t, rw), o_b_p.reshape(bp * t, mbw), w, lay)
    mem_n = _rmsnorm(mem_prompt.reshape(bp * n_mem, d), norm_mem[0], BF16)
    mk_p = _matmul(mem_n, wx_k[0])
    mv_p = _matmul(mem_n, wx_v[0])
    x2_p = _after_mix(x1_p, mk_p.reshape(bp, n_mem, xw), mv_p.reshape(bp, n_mem, xw), bp, t, w)
    hid_p, cv_p = _ffn_in_seq(_rmsnorm(x2_p, norm_ffn[0], BF16).reshape(bp, t, d), w_gate[0], w_up[0],
                              conv_w[0], conv_b[0], jnp.zeros((bp, CONV_W - 1, f), F32))
    x3_p = _matmul(hid_p.reshape(bp * t, f), w_down_b, res=x2_p,
                   tm=_pick(bp * t, (512, 256, 128, 64, 32, 16, 8)))
    y_p = _rmsnorm(x3_p, norm_out, F32).reshape(bp, t, d)

    xs = x_sample.reshape(db, d)
    proj_s = _project(_rmsnorm(xs, norm_mix[0], BF16), w_in_t, lay)
    streams_s = _rwkv_prep(proj_s.reshape(db, 1, lay["np"]), _split_prev(state_rwkv_shift[0], lay),
                           lay, prm)
    streams_s = [jnp.pad(s, ((0, 0), (0, STEP_CHUNK - 1), (0, 0))) for s in streams_s]
    o_a_s, sw_s = _rwkv_scan(streams_s, state_rwkv_wkv[0], prm, STEP_CHUNK)
    o_a_s = o_a_s[:, 0, :]
    q_s, k_rot_s = _rope_rows(proj_s, lay, past_len)
    v_s = proj_s[:, lay["off_mv"]:lay["off_mv"] + mbw]
    heads3 = lambda z: z.reshape(db, nh_mb, MB_HEAD)
    sel = _decode_select(heads3(q_s), page_sums.reshape(db, n_pages, nh_mb, MB_HEAD))
    sel = sel.reshape(db, MB_TOPK, nh_mb).transpose(0, 2, 1)
    o_b_s = _decode_attend(heads3(q_s), heads3(k_rot_s), heads3(v_s), sel, page_table,
                           cache_moba_k[0], cache_moba_v[0])
    x1_s = _mix_and_project(xs, proj_s, o_a_s, o_b_s.astype(BF16), w, lay)
    x2_s = _after_mix(x1_s, cache_mem_k[0].reshape(db, n_mem, xw), cache_mem_v[0].reshape(db, n_mem, xw),
                      db, 1, w)
    conv_prev_s = state_ffn_conv[0]
    hid_s, gt_s = _ffn_in_step(_rmsnorm(x2_s, norm_ffn[0], BF16), w_gate[0], w_up[0], conv_w[0],
                               conv_b[0], conv_prev_s[:, 0], conv_prev_s[:, 1])
    x3_s = _matmul(hid_s, w_down_b, res=x2_s)
    y_s = _rmsnorm(x3_s, norm_out, F32).reshape(db, 1, d)
    cv_s = jnp.stack([conv_prev_s[:, 1], gt_s], axis=1)

    return (y_p, y_s, k_rot_p[None], v_p[None],
            k_rot_s.reshape(1, db, 1, nh_mb, MB_HEAD), v_s.reshape(1, db, 1, nh_mb, MB_HEAD),
            mk_p.reshape(1, bp, n_mem, xw // X_HEAD, X_HEAD), mv_p.reshape(1, bp, n_mem, xw // X_HEAD, X_HEAD),
            sw_p[None], sw_s[None],
            _unpermute_last(proj_p3[:, t - 1], lay)[None], _unpermute_last(proj_s, lay)[None],
            cv_p[None], cv_s[None])
```

```python
import functools
import math

import jax
import jax.numpy as jnp
from jax import lax
from jax.experimental import pallas as pl
from jax.experimental.pallas import tpu as pltpu

F32 = jnp.float32
BF16 = jnp.bfloat16

NORM_EPS = 1e-5
RW_HEAD = 64
RW_GN_EPS = 64e-5
MB_HEAD = 128
MB_BLOCK = 256
MB_TOPK = 3
PAGE_SIZE = 128
ROPE_DIM = MB_HEAD // 4
ROPE_THETA = 500000.0
X_HEAD = 128
CONV_W = 3

LANE = 128
VMEM_LIMIT = 56 * 1024 * 1024
SCAN_CHUNK = 64
SCAN_HEADS = 16
STEP_CHUNK = 8


def _params(sem):
    return pltpu.CompilerParams(dimension_semantics=sem, vmem_limit_bytes=VMEM_LIMIT)


def _pick(n, prefs):
    for p in prefs:
        if n % p == 0:
            return p
    return n


def _dot(a, b, dims=(((1,), (0,)), ((), ())), exact=False):
    if exact:
        return lax.dot_general(a, b, dims, precision=lax.Precision.HIGHEST,
                               preferred_element_type=F32)
    return lax.dot_general(a.astype(BF16), b.astype(BF16), dims, preferred_element_type=F32)


_NT = (((1,), (1,)), ((), ()))
_TN = (((0,), (0,)), ((), ()))


def _sigmoid(x):
    return 1.0 / (1.0 + jnp.exp(-x))


def _rmsnorm_kernel(x_ref, g_ref, o_ref):
    x = x_ref[...]
    y = x * lax.rsqrt(jnp.mean(x * x, axis=-1, keepdims=True) + NORM_EPS)
    o_ref[...] = (y * g_ref[...]).astype(o_ref.dtype)


def _rmsnorm(x, g, out_dtype):
    m, d = x.shape
    tr = _pick(m, (256, 128, 64, 32, 16, 8))
    return pl.pallas_call(
        _rmsnorm_kernel,
        grid=(m // tr,),
        in_specs=[pl.BlockSpec((tr, d), lambda i: (i, 0)),
                  pl.BlockSpec((1, d), lambda i: (0, 0))],
        out_specs=pl.BlockSpec((tr, d), lambda i: (i, 0)),
        out_shape=jax.ShapeDtypeStruct((m, d), out_dtype),
        compiler_params=_params(("parallel",)),
        name="rmsnorm",
    )(x, g.reshape(1, d))


def _mm_kernel(a_ref, w_ref, *refs):
    o_ref = refs[-1]
    out = _dot(a_ref[...], w_ref[...])
    if len(refs) == 2:
        out = out + refs[0][...]
    o_ref[...] = out.astype(o_ref.dtype)


def _matmul(a, w, res=None, out_dtype=F32, tm=None, tn=256):
    m, kdim = a.shape
    n = w.shape[1]
    tm = tm or _pick(m, (1024, 512, 256, 128, 64, 32, 16, 8))
    tn = min(tn, n)
    assert m % tm == 0
    in_specs = [pl.BlockSpec((tm, kdim), lambda i, j: (i, 0)),
                pl.BlockSpec((kdim, tn), lambda i, j: (0, j))]
    args = [a, w]
    if res is not None:
        in_specs.append(pl.BlockSpec((tm, tn), lambda i, j: (i, j)))
        args.append(res)
    return pl.pallas_call(
        _mm_kernel,
        grid=(m // tm, pl.cdiv(n, tn)),
        in_specs=in_specs,
        out_specs=pl.BlockSpec((tm, tn), lambda i, j: (i, j)),
        out_shape=jax.ShapeDtypeStruct((m, n), out_dtype),
        compiler_params=_params(("parallel", "parallel")),
        name="matmul",
    )(*args)


def _proj_kernel(a_ref, as_ref, wt_ref, o_ref, os_ref, wb):
    @pl.when(pl.program_id(1) == 0)
    def _():
        wb[...] = wt_ref[...].astype(BF16)
        os_ref[...] = lax.dot_general(as_ref[...], wb[...], _NT, preferred_element_type=F32)

    o_ref[...] = lax.dot_general(a_ref[...], wb[...], _NT, preferred_element_type=F32)


def _project(xn, xn_s, w_in_t, lay):
    m, kdim = xn.shape
    ms = xn_s.shape[0]
    n_src = w_in_t.shape[0]
    tn, shift = lay["tn"], lay["shift"]
    first_shifted = lay["off_q"] // tn
    tm = _pick(m, (1024, 512, 256, 128, 64, 32, 16, 8))
    sub = 8
    assert lay["np"] % tn == 0 and lay["np"] == n_src + shift and shift % sub == 0
    return pl.pallas_call(
        _proj_kernel,
        grid=(lay["np"] // tn, m // tm),
        in_specs=[pl.BlockSpec((tm, kdim), lambda j, i: (i, 0)),
                  pl.BlockSpec((ms, kdim), lambda j, i: (0, 0)),
                  pl.BlockSpec((pl.Element(tn), pl.Element(kdim)),
                               lambda j, i: ((j * (tn // sub)
                                              - jnp.where(j >= first_shifted, shift // sub, 0)) * sub, 0))],
        out_specs=[pl.BlockSpec((tm, tn), lambda j, i: (i, j)),
                   pl.BlockSpec((ms, tn), lambda j, i: (0, j))],
        out_shape=[jax.ShapeDtypeStruct((m, lay["np"]), F32),
                   jax.ShapeDtypeStruct((ms, lay["np"]), F32)],
        scratch_shapes=[pltpu.VMEM((tn, kdim), BF16)],
        compiler_params=_params(("parallel", "arbitrary")),
        name="in_projection",
    )(xn, xn_s, w_in_t)


def _prep_kernel(zr_ref, zk_ref, zv_ref, zl_ref, pr_ref, pk_ref, pv_ref, pl_ref,
                 mur_ref, muk_ref, muv_ref, mul_ref, w0_ref, a0_ref, kk_ref, ka_ref,
                 w2_ref, a2_ref, g2_ref,
                 r_ref, lw_ref, k_ref, v_ref, kkr_ref, a_ref, g_ref,
                 cr, ck, cv, cl, *, tp):
    t = pl.program_id(1)

    @pl.when(t == 0)
    def _():
        cr[...] = pr_ref[...]
        ck[...] = pk_ref[...]
        cv[...] = pv_ref[...]
        cl[...] = pl_ref[...]

    def shift_mix(z_ref, carry, mu_ref):
        z = z_ref[...]
        if tp == 1:
            prev = carry[...]
        else:
            first = lax.broadcasted_iota(jnp.int32, z.shape, 0) == 0
            prev = jnp.where(first, carry[...], pltpu.roll(z, 1, 0))
        carry[...] = z[tp - 1:tp, :]
        return z + mu_ref[...] * (prev - z)

    r = shift_mix(zr_ref, cr, mur_ref)
    k = shift_mix(zk_ref, ck, muk_ref)
    v = shift_mix(zv_ref, cv, muv_ref)
    xl = shift_mix(zl_ref, cl, mul_ref)
    u = w0_ref[...] + _dot(jnp.tanh(xl), w2_ref[...])
    lw = -math.exp(-0.5) * _sigmoid(u)
    a = _sigmoid(a0_ref[...] + _dot(xl, a2_ref[...]))
    g = _dot(_sigmoid(xl), g2_ref[...])
    r_ref[...] = r
    lw_ref[...] = lw
    k_ref[...] = k * (1.0 + (a - 1.0) * ka_ref[...])
    v_ref[...] = v
    kkr_ref[...] = k * kk_ref[...]
    a_ref[...] = a
    g_ref[...] = g


def _rwkv_prep(proj, prev, lay, prm):
    b, t, _ = proj.shape
    rw, lp = lay["rw"], lay["lp"]
    tp = _pick(t, (128, 64, 32, 16, 8))
    assert lay["off_l"] % lp == 0
    lblk = lay["off_l"] // lp
    row = lambda j: pl.BlockSpec((None, tp, rw), lambda bi, ti, j=j: (bi, ti, j))
    prev_spec = lambda w: pl.BlockSpec((None, 1, w), lambda bi, ti: (bi, 0, 0))
    vec = lambda w: pl.BlockSpec((1, w), lambda bi, ti: (0, 0))
    mat = pl.BlockSpec((lp, rw), lambda bi, ti: (0, 0))
    out_spec = pl.BlockSpec((None, tp, rw), lambda bi, ti: (bi, ti, 0))
    out_sds = jax.ShapeDtypeStruct((b, t, rw), F32)
    return pl.pallas_call(
        functools.partial(_prep_kernel, tp=tp),
        grid=(b, t // tp),
        in_specs=[row(0), row(1), row(2),
                  pl.BlockSpec((None, tp, lp), lambda bi, ti: (bi, ti, lblk)),
                  prev_spec(rw), prev_spec(rw), prev_spec(rw), prev_spec(lp),
                  vec(rw), vec(rw), vec(rw), vec(lp), vec(rw), vec(rw), vec(rw), vec(rw),
                  mat, mat, mat],
        out_specs=[out_spec] * 7,
        out_shape=[out_sds] * 7,
        scratch_shapes=[pltpu.VMEM((1, rw), F32), pltpu.VMEM((1, rw), F32),
                        pltpu.VMEM((1, rw), F32), pltpu.VMEM((1, lp), F32)],
        compiler_params=_params(("parallel", "arbitrary")),
        name="rwkv_prep",
    )(proj, proj, proj, proj, *prev,
      prm["mu_r"], prm["mu_k"], prm["mu_v"], prm["mu_l"], prm["w0"], prm["a0"],
      prm["k_k"], prm["k_a"], prm["w2p"], prm["a2p"], prm["g2p"])


def _scan_kernel(*refs, chunk, heads, n_chunks, n_side):
    if n_side:
        refs = refs[1:]
    (r_ref, lw_ref, k_ref, v_ref, kkr_ref, a_ref, g_ref, rk_ref, lnw_ref, lnb_ref, s0_ref) = refs[:11]
    page_refs = refs[11:11 + n_side]
    o_ref, sf_ref = refs[11 + n_side:13 + n_side]
    if n_side:
        side_ref = refs[13 + n_side]
        for s in range(n_side):
            side_ref[s] = jnp.sum(page_refs[s][...], axis=0)
    s_scr = refs[-1]
    c = pl.program_id(2)

    @pl.when(c == 0)
    def _():
        s_scr[...] = s0_ref[...]

    n = chunk
    row = lax.broadcasted_iota(jnp.int32, (n, n), 0)
    col = lax.broadcasted_iota(jnp.int32, (n, n), 1)
    incl = row >= col
    strict = row > col
    eye = (row == col).astype(F32)

    lw = lw_ref[...]
    cum = _dot(incl.astype(F32), lw, exact=True)
    cum_prev = cum - lw
    total = cum[n - 1:n, :]
    e_cum = jnp.exp(cum)
    e_prev = jnp.exp(cum_prev)
    e_neg = jnp.exp(-cum)
    e_rest = jnp.exp(total - cum)
    e_total = jnp.exp(total)

    r_all, k_all, v_all = r_ref[...], k_ref[...], v_ref[...]
    kkr_all, a_all, g_all = kkr_ref[...], a_ref[...], g_ref[...]
    rk_all, lnw_all, lnb_all = rk_ref[...], lnw_ref[...], lnb_ref[...]

    hs = range(heads)
    sls = [slice(h * RW_HEAD, (h + 1) * RW_HEAD) for h in hs]
    r = [r_all[:, sl] for sl in sls]
    k = [k_all[:, sl] for sl in sls]
    v = [v_all[:, sl] for sl in sls]
    kk, bb, q2 = [], [], []
    for h, sl in enumerate(sls):
        kkr = kkr_all[:, sl]
        kk_h = kkr / jnp.maximum(jnp.sqrt(jnp.sum(kkr * kkr, axis=-1, keepdims=True)), 1e-12)
        kk.append(kk_h)
        bb.append(kk_h * a_all[:, sl])
        q2.append(jnp.concatenate([kk_h * e_prev[:, sl], r[h] * e_cum[:, sl]], axis=0))
    p_k = [_dot(q2[h], k[h] * e_neg[:, sls[h]], _NT) for h in hs]
    p_b = [_dot(q2[h], bb[h] * e_neg[:, sls[h]], _NT) for h in hs]
    s = [s_scr[h] for h in hs]
    qs = [_dot(q2[h], s[h], _NT) for h in hs]
    m_k = [jnp.where(strict, p[:n], 0.0) for p in p_k]
    a_k = [jnp.where(incl, p[n:], 0.0) for p in p_k]
    a_b = [jnp.where(incl, p[n:], 0.0) for p in p_b]
    npow = [jnp.where(strict, -p[:n], 0.0) for p in p_b]
    tinv = [eye + x for x in npow]
    for _ in range(int(math.log2(n)) - 1):
        npow = [_dot(x, x) for x in npow]
        tinv = [t + _dot(t, x) for t, x in zip(tinv, npow)]
    rhs = [qs[h][:n] + _dot(m_k[h], v[h]) for h in hs]
    u = [_dot(tinv[h], rhs[h]) for h in hs]
    y = [qs[h][n:] + _dot(a_k[h], v[h]) - _dot(a_b[h], u[h]) for h in hs]
    for h, sl in enumerate(sls):
        s_scr[h] = (s[h] * e_total[:, sl] + _dot(v[h], k[h] * e_rest[:, sl], _TN)
                    - _dot(u[h], bb[h] * e_rest[:, sl], _TN))
    outs = []
    for h, sl in enumerate(sls):
        mu = jnp.mean(y[h], axis=-1, keepdims=True)
        yc = y[h] - mu
        var = jnp.mean(yc * yc, axis=-1, keepdims=True)
        yn = yc * lax.rsqrt(var + RW_GN_EPS) * lnw_all[:, sl] + lnb_all[:, sl]
        bonus = jnp.sum(r[h] * k[h] * rk_all[:, sl], axis=-1, keepdims=True) * v[h]
        outs.append((yn + bonus) * g_all[:, sl])
    o_ref[...] = jnp.concatenate(outs, axis=1).astype(o_ref.dtype)

    @pl.when(c == n_chunks - 1)
    def _():
        sf_ref[...] = s_scr[...]


def _rwkv_scan(streams, s0, prm, chunk, paged_keys=None):
    b, t, rw = streams[0].shape
    nh = rw // RW_HEAD
    hg = _pick(nh, (SCAN_HEADS, 4, 2))
    w = hg * RW_HEAD
    assert t % chunk == 0 and chunk & (chunk - 1) == 0 and w % LANE == 0
    n_chunks, n_groups = t // chunk, nh // hg
    stream_spec = pl.BlockSpec((None, chunk, w), lambda bi, gi, ci, *_: (bi, ci, gi))
    vec_spec = pl.BlockSpec((1, w), lambda bi, gi, ci, *_: (0, gi))
    state_spec = pl.BlockSpec((None, hg, RW_HEAD, RW_HEAD), lambda bi, gi, ci, *_: (bi, gi, 0, 0))
    in_specs = [stream_spec] * 7 + [vec_spec] * 3 + [state_spec]
    out_specs = [stream_spec, state_spec]
    out_shape = [jax.ShapeDtypeStruct((b, t, rw), BF16),
                 jax.ShapeDtypeStruct((b, nh, RW_HEAD, RW_HEAD), F32)]
    args = [*streams, prm["r_k"], prm["ln_w"], prm["ln_b"], s0]
    n_side, n_prefetch = 0, 0
    if paged_keys is not None:
        cache_k, page_ids = paged_keys
        _, page, nhk, d = cache_k.shape
        n_ids = page_ids.shape[0]
        n_steps = b * n_groups * n_chunks
        n_side = pl.cdiv(n_ids, n_steps)
        step = lambda bi, gi, ci: (bi * n_groups + gi) * n_chunks + ci

        def page_spec(s):
            return pl.BlockSpec(
                (None, page, nhk, d),
                lambda bi, gi, ci, ids: (ids[jnp.minimum(step(bi, gi, ci) * n_side + s, n_ids - 1)], 0, 0, 0))

        in_specs += [page_spec(s) for s in range(n_side)]
        out_specs.append(pl.BlockSpec((n_side, nhk, d), lambda bi, gi, ci, ids: (step(bi, gi, ci), 0, 0)))
        out_shape.append(jax.ShapeDtypeStruct((n_steps * n_side, nhk, d), F32))
        args = [page_ids] + args + [cache_k] * n_side
        n_prefetch = 1
    outs = pl.pallas_call(
        functools.partial(_scan_kernel, chunk=chunk, heads=hg, n_chunks=n_chunks, n_side=n_side),
        grid_spec=pltpu.PrefetchScalarGridSpec(
            num_scalar_prefetch=n_prefetch,
            grid=(b, n_groups, n_chunks),
            in_specs=in_specs,
            out_specs=out_specs,
            scratch_shapes=[pltpu.VMEM((hg, RW_HEAD, RW_HEAD), F32)],
        ),
        out_shape=out_shape,
        compiler_params=_params(("parallel", "parallel", "arbitrary")),
        name="rwkv_scan",
    )(*args)
    if paged_keys is not None:
        return outs[0], outs[1], outs[2][:n_ids]
    return outs[0], outs[1]


def _rope_tables(pos):
    half = ROPE_DIM // 2
    inv = 1.0 / (ROPE_THETA ** (jnp.arange(0, ROPE_DIM, 2, dtype=F32) / ROPE_DIM))
    ang = pos.astype(F32)[:, None] * inv[None, :]
    cos, sin = jnp.cos(ang), jnp.sin(ang)
    t = pos.shape[0]
    rest = MB_HEAD - ROPE_DIM
    c = jnp.concatenate([cos, cos, jnp.ones((t, rest), F32)], axis=1)
    s = jnp.concatenate([-sin, sin, jnp.zeros((t, rest), F32)], axis=1)
    del half
    return c, s


def _rope(x, c, s):
    half = ROPE_DIM // 2
    lane = lax.broadcasted_iota(jnp.int32, x.shape, 1)
    swapped = jnp.where(lane < half, pltpu.roll(x, MB_HEAD - half, 1), pltpu.roll(x, half, 1))
    return x * c + swapped * s


def _moba_prompt_kernel(q_ref, k_ref, v_ref, c_ref, s_ref, kout_hbm, vout_hbm, o_ref,
                        kbuf, vbuf, sems, *, nb):
    bi, hi = pl.program_id(0), pl.program_id(1)
    c, s = c_ref[...], s_ref[...]
    q = _rope(q_ref[...], c, s)
    k = _rope(k_ref[...], c, s)
    kbuf[...] = k
    vbuf[...] = v_ref[...]
    k_copy = pltpu.make_async_copy(kbuf, kout_hbm.at[bi, :, hi, :], sems.at[0])
    v_copy = pltpu.make_async_copy(vbuf, vout_hbm.at[bi, :, hi, :], sems.at[1])
    k_copy.start()
    v_copy.start()
    t = q.shape[0]
    k_means = jnp.sum(k.reshape(nb, MB_BLOCK, MB_HEAD), axis=1) * (1.0 / MB_BLOCK)
    nbp = -(-nb // 8) * 8
    if nbp > nb:
        k_means = jnp.concatenate([k_means, jnp.zeros((nbp - nb, MB_HEAD), F32)], axis=0)
    scores = _dot(k_means, q, _NT, exact=True)
    blk = lax.broadcasted_iota(jnp.int32, (nbp, t), 0)
    tpos = lax.broadcasted_iota(jnp.int32, (nbp, t), 1)
    rank = jnp.zeros((nbp, t), jnp.int32)
    for m in range(nb):
        sm = scores[m:m + 1, :]
        ahead = (sm > scores) | ((sm == scores) & (blk > m))
        rank = rank + jnp.where(ahead & (tpos >= (m + 1) * MB_BLOCK), 1, 0)
    sel = (rank < MB_TOPK) & (tpos >= (blk + 1) * MB_BLOCK)
    eye = (lax.broadcasted_iota(jnp.int32, (nbp, nbp), 0)
           == lax.broadcasted_iota(jnp.int32, (nbp, nbp), 1)).astype(F32)
    sel_rows = _dot(jnp.where(sel, 1.0, 0.0), eye, _TN)
    qb = q.astype(BF16)
    kb = k.astype(BF16)
    vb = v_ref[...].astype(BF16)
    scale = MB_HEAD ** -0.5
    rowi = lax.broadcasted_iota(jnp.int32, (MB_BLOCK, MB_BLOCK), 0)
    coli = lax.broadcasted_iota(jnp.int32, (MB_BLOCK, MB_BLOCK), 1)
    causal = rowi >= coli
    neg_inf = float("-inf")
    for i in range(nb):
        rows = slice(i * MB_BLOCK, (i + 1) * MB_BLOCK)
        q_i = qb[rows]
        logits = []
        for j in range(i + 1):
            cols = slice(j * MB_BLOCK, (j + 1) * MB_BLOCK)
            lg = _dot(q_i, kb[cols], _NT) * scale
            if j == i:
                mask = causal
            elif i <= MB_TOPK:
                mask = None
            else:
                mask = sel_rows[rows, j:j + 1] > 0.5
            if mask is not None:
                lg = jnp.where(mask, lg, neg_inf)
            logits.append(lg)
        mx = logits[0].max(axis=-1, keepdims=True)
        for lg in logits[1:]:
            mx = jnp.maximum(mx, lg.max(axis=-1, keepdims=True))
        den = jnp.zeros((MB_BLOCK, 1), F32)
        acc = jnp.zeros((MB_BLOCK, MB_HEAD), F32)
        for j, lg in enumerate(logits):
            pj = jnp.exp(lg - mx)
            den = den + jnp.sum(pj, axis=-1, keepdims=True)
            acc = acc + _dot(pj, vb[j * MB_BLOCK:(j + 1) * MB_BLOCK])
        o_ref[rows, :] = (acc / den).astype(o_ref.dtype)
    del t
    k_copy.wait()
    v_copy.wait()


def _moba_prompt(proj, lay):
    b, t, _ = proj.shape
    assert t % MB_BLOCK == 0
    mbw = lay["mb"]
    nh = mbw // MB_HEAD
    ctab, stab = _rope_tables(jnp.arange(t))
    qb, kb, vb = lay["off_q"] // MB_HEAD, lay["off_mk"] // MB_HEAD, lay["off_mv"] // MB_HEAD
    head = lambda off: pl.BlockSpec((None, t, MB_HEAD), lambda bi, hi, off=off: (bi, 0, off + hi))
    tab = pl.BlockSpec((t, MB_HEAD), lambda bi, hi: (0, 0))
    hbm = pl.BlockSpec(memory_space=pl.ANY)
    rows_sds = jax.ShapeDtypeStruct((b, t, nh, MB_HEAD), F32)
    return pl.pallas_call(
        functools.partial(_moba_prompt_kernel, nb=t // MB_BLOCK),
        grid=(b, nh),
        in_specs=[head(qb), head(kb), head(vb), tab, tab],
        out_specs=[hbm, hbm, pl.BlockSpec((None, t, MB_HEAD), lambda bi, hi: (bi, 0, hi))],
        out_shape=[rows_sds, rows_sds, jax.ShapeDtypeStruct((b, t, mbw), BF16)],
        scratch_shapes=[pltpu.VMEM((t, MB_HEAD), F32), pltpu.VMEM((t, MB_HEAD), F32),
                        pltpu.SemaphoreType.DMA((2,))],
        compiler_params=_params(("parallel", "parallel")),
        name="moba_prompt",
    )(proj, proj, proj, ctab, stab)


def _rope_rows_kernel(q_ref, k_ref, c_ref, s_ref, qo_ref, ko_ref, *, heads):
    c, s = c_ref[...], s_ref[...]
    for h in range(heads):
        sl = slice(h * MB_HEAD, (h + 1) * MB_HEAD)
        qo_ref[:, sl] = _rope(q_ref[:, sl], c, s)
        ko_ref[:, sl] = _rope(k_ref[:, sl], c, s)


def _rope_rows(proj2d, lay, pos):
    m = proj2d.shape[0]
    mbw = lay["mb"]
    ctab, stab = _rope_tables(jnp.full((1,), pos))
    cw = math.gcd(math.gcd(lay["off_q"], lay["off_mk"]), mbw)
    assert cw % MB_HEAD == 0
    spec = lambda off: pl.BlockSpec((m, cw), lambda i, off=off: (0, off // cw + i))
    tab = pl.BlockSpec((1, MB_HEAD), lambda i: (0, 0))
    out = pl.BlockSpec((m, cw), lambda i: (0, i))
    sds = jax.ShapeDtypeStruct((m, mbw), F32)
    return pl.pallas_call(
        functools.partial(_rope_rows_kernel, heads=cw // MB_HEAD),
        grid=(mbw // cw,),
        in_specs=[spec(lay["off_q"]), spec(lay["off_mk"]), tab, tab],
        out_specs=[out, out],
        out_shape=[sds, sds],
        compiler_params=_params(("arbitrary",)),
        name="rope_rows",
    )(proj2d, proj2d, ctab, stab)


def _decode_select_kernel(q_ref, ps_ref, idx_ref, *, nb, ppb):
    q = q_ref[...]
    page_sums = ps_ref[...]
    block_sums = jnp.sum(page_sums.reshape((nb, ppb) + page_sums.shape[1:]), axis=1)
    scores = jnp.sum(block_sums * (1.0 / MB_BLOCK) * q[None], axis=-1, keepdims=True)
    blk = lax.broadcasted_iota(jnp.int32, scores.shape, 0)
    rank = jnp.zeros(scores.shape, jnp.int32)
    for m in range(nb):
        sm = scores[m:m + 1]
        ahead = (sm > scores) | ((sm == scores) & (blk > m))
        rank = rank + ahead.astype(jnp.int32)
    for r in range(MB_TOPK):
        idx_ref[r] = jnp.sum(jnp.where(rank == r, blk, 0), axis=0)


def _decode_select(q, page_sums):
    db, nh, d = q.shape
    n_pages = page_sums.shape[1]
    ppb = MB_BLOCK // PAGE_SIZE
    nb = n_pages // ppb
    assert nb >= MB_TOPK
    return pl.pallas_call(
        functools.partial(_decode_select_kernel, nb=nb, ppb=ppb),
        grid=(db,),
        in_specs=[pl.BlockSpec((None, nh, d), lambda bi: (bi, 0, 0)),
                  pl.BlockSpec((None, n_pages, nh, d), lambda bi: (bi, 0, 0, 0))],
        out_specs=pl.BlockSpec((None, MB_TOPK, nh, 1), lambda bi: (bi, 0, 0, 0)),
        out_shape=jax.ShapeDtypeStruct((db, MB_TOPK, nh, 1), jnp.int32),
        compiler_params=_params(("parallel",)),
        name="moba_decode_select",
    )(q, page_sums)


def _decode_attend_kernel(sel_ref, pt_ref, q_ref, kn_ref, vn_ref, ck_hbm, cv_hbm, o_ref,
                          kbuf, vbuf, sems, *, nh, ppb):
    bi = pl.program_id(0)
    n_sel = MB_TOPK * ppb
    page = kbuf.shape[2]

    def head_copies(h):
        copies = []
        for r in range(MB_TOPK):
            blk = sel_ref[(bi * nh + h) * MB_TOPK + r]
            for j in range(ppb):
                phys = pt_ref[bi, blk * ppb + j]
                s = r * ppb + j
                copies.append(pltpu.make_async_copy(ck_hbm.at[phys, :, h, :], kbuf.at[h, s], sems.at[h]))
                copies.append(pltpu.make_async_copy(cv_hbm.at[phys, :, h, :], vbuf.at[h, s], sems.at[h]))
        return copies

    all_copies = [head_copies(h) for h in range(nh)]
    for copies in all_copies:
        for cp in copies:
            cp.start()

    scale = MB_HEAD ** -0.5
    for h in range(nh):
        for cp in all_copies[h]:
            cp.wait()
        q = q_ref[h:h + 1, :]
        keys = kbuf[h].reshape(n_sel * page, MB_HEAD)
        vals = vbuf[h].reshape(n_sel * page, MB_HEAD)
        lg = _dot(jnp.broadcast_to(q, (8, MB_HEAD)), keys, _NT)[0:1] * scale
        own = jnp.sum(q * kn_ref[h:h + 1, :], axis=-1, keepdims=True) * scale
        mx = jnp.maximum(own, lg.max(axis=-1, keepdims=True))
        p_own = jnp.exp(own - mx)
        p = jnp.exp(lg - mx)
        den = p_own + jnp.sum(p, axis=-1, keepdims=True)
        acc = p_own * vn_ref[h:h + 1, :] + _dot(jnp.broadcast_to(p, (8, p.shape[1])), vals)[0:1]
        o_ref[h:h + 1, :] = acc / den


def _decode_attend(q, k_new, v_new, sel, page_table, cache_k, cache_v):
    db, nh, d = q.shape
    page = cache_k.shape[1]
    ppb = MB_BLOCK // page
    n_sel = MB_TOPK * ppb
    vec = pl.BlockSpec((None, nh, d), lambda bi, sel_r, pt_r: (bi, 0, 0))
    hbm = pl.BlockSpec(memory_space=pl.ANY)
    out = pl.pallas_call(
        functools.partial(_decode_attend_kernel, nh=nh, ppb=ppb),
        grid_spec=pltpu.PrefetchScalarGridSpec(
            num_scalar_prefetch=2,
            grid=(db,),
            in_specs=[vec, vec, vec, hbm, hbm],
            out_specs=vec,
            scratch_shapes=[pltpu.VMEM((nh, n_sel, page, d), F32),
                            pltpu.VMEM((nh, n_sel, page, d), F32),
                            pltpu.SemaphoreType.DMA((nh,))],
        ),
        out_shape=jax.ShapeDtypeStruct((db, nh, d), F32),
        compiler_params=_params(("arbitrary",)),
        name="moba_decode_attend",
    )(sel.reshape(-1), page_table, q, k_new, v_new, cache_k, cache_v)
    return out.reshape(db, nh * d)


def _merge_kernel(oa_ref, ob_ref, wa_ref, wb_ref, ga_ref, gb_ref, o_ref):
    ya = _dot(oa_ref[...], wa_ref[...])
    yb = _dot(ob_ref[...], wb_ref[...])
    o_ref[...] = (_sigmoid(ga_ref[...]) * ya + _sigmoid(gb_ref[...]) * yb).astype(o_ref.dtype)


def _merge(o_a, o_b, w_a, w_b, proj2d, lay):
    m, ka = o_a.shape
    kb = o_b.shape[1]
    d = w_a.shape[1]
    tm = _pick(m, (1024, 512, 256, 128, 64, 32, 16, 8))
    tn = next(c for c in (512, 256, LANE)
              if d % c == 0 and lay["off_ga"] % c == 0 and lay["off_gb"] % c == 0)
    ga, gb = lay["off_ga"] // tn, lay["off_gb"] // tn
    return pl.pallas_call(
        _merge_kernel,
        grid=(m // tm, d // tn),
        in_specs=[pl.BlockSpec((tm, ka), lambda i, j: (i, 0)),
                  pl.BlockSpec((tm, kb), lambda i, j: (i, 0)),
                  pl.BlockSpec((ka, tn), lambda i, j: (0, j)),
                  pl.BlockSpec((kb, tn), lambda i, j: (0, j)),
                  pl.BlockSpec((tm, tn), lambda i, j: (i, ga + j)),
                  pl.BlockSpec((tm, tn), lambda i, j: (i, gb + j))],
        out_specs=pl.BlockSpec((tm, tn), lambda i, j: (i, j)),
        out_shape=jax.ShapeDtypeStruct((m, d), BF16),
        compiler_params=_params(("parallel", "parallel")),
        name="gated_merge",
    )(o_a, o_b, w_a, w_b, proj2d, proj2d)


def _xattn_kernel(q_ref, k_ref, v_ref, o_ref, *, heads):
    scale = X_HEAD ** -0.5
    for h in range(heads):
        sl = slice(h * X_HEAD, (h + 1) * X_HEAD)
        q = q_ref[:, sl]
        if q.shape[0] < 8:
            q = jnp.broadcast_to(q, (8, X_HEAD))
        lg = _dot(q, k_ref[:, sl], _NT) * scale
        p = jnp.exp(lg - lg.max(axis=-1, keepdims=True))
        o = _dot(p, v_ref[:, sl]) / jnp.sum(p, axis=-1, keepdims=True)
        o_ref[:, sl] = o[:o_ref.shape[0]].astype(o_ref.dtype)


def _xattn(q, mk, mv):
    b, t, xw = q.shape
    n_mem = mk.shape[1]
    tq = _pick(t, (512, 256, 128, 64, 32, 16, 8))
    qspec = pl.BlockSpec((None, tq, xw), lambda bi, ti: (bi, ti, 0))
    mspec = pl.BlockSpec((None, n_mem, xw), lambda bi, ti: (bi, 0, 0))
    return pl.pallas_call(
        functools.partial(_xattn_kernel, heads=xw // X_HEAD),
        grid=(b, t // tq),
        in_specs=[qspec, mspec, mspec],
        out_specs=qspec,
        out_shape=jax.ShapeDtypeStruct((b, t, xw), BF16),
        compiler_params=_params(("parallel", "parallel")),
        name="cross_attention",
    )(q, mk, mv)


def _xblock_kernel(x_ref, gx_ref, wq_ref, wo_ref, k_ref, v_ref, gf_ref, x2_ref, xn_ref, *, heads):
    x = x_ref[...]
    xn = x * lax.rsqrt(jnp.mean(x * x, axis=-1, keepdims=True) + NORM_EPS) * gx_ref[...]
    q = _dot(xn, wq_ref[...])
    scale = X_HEAD ** -0.5
    outs = []
    for h in range(heads):
        sl = slice(h * X_HEAD, (h + 1) * X_HEAD)
        lg = _dot(q[:, sl], k_ref[:, sl], _NT) * scale
        p = jnp.exp(lg - lg.max(axis=-1, keepdims=True))
        outs.append(_dot(p, v_ref[:, sl]) / jnp.sum(p, axis=-1, keepdims=True))
    x2 = x + _dot(jnp.concatenate(outs, axis=1), wo_ref[...])
    x2_ref[...] = x2
    xn_ref[...] = (x2 * lax.rsqrt(jnp.mean(x2 * x2, axis=-1, keepdims=True) + NORM_EPS)
                   * gf_ref[...]).astype(xn_ref.dtype)


def _xblock(x, g_x, wq, wo, mk, mv, g_next):
    b, t, d = x.shape
    xw = wq.shape[1]
    n_mem = mk.shape[1]
    tq = _pick(t, (256, 128, 64, 32, 16, 8))
    rows = pl.BlockSpec((None, tq, d), lambda bi, ti: (bi, ti, 0))
    vec = pl.BlockSpec((1, d), lambda bi, ti: (0, 0))
    mem = pl.BlockSpec((None, n_mem, xw), lambda bi, ti: (bi, 0, 0))
    return pl.pallas_call(
        functools.partial(_xblock_kernel, heads=xw // X_HEAD),
        grid=(b, t // tq),
        in_specs=[rows, vec, pl.BlockSpec((d, xw), lambda bi, ti: (0, 0)),
                  pl.BlockSpec((xw, d), lambda bi, ti: (0, 0)), mem, mem, vec],
        out_specs=[rows, rows],
        out_shape=[jax.ShapeDtypeStruct((b, t, d), F32), jax.ShapeDtypeStruct((b, t, d), BF16)],
        compiler_params=_params(("parallel", "parallel")),
        name="cross_attention_block",
    )(x, g_x.reshape(1, d), wq, wo, mk, mv, g_next.reshape(1, d))


def _ffn_in_kernel(x_ref, xs_ref, wg_ref, wu_ref, cw_ref, cb_ref, prev_ref, p0_ref, p1_ref,
                   h_ref, cn_ref, hs_ref, gs_ref, carry, wg_b, wu_b, *, tm, n_t):
    t = pl.program_id(2)

    @pl.when((pl.program_id(1) == 0) & (t == 0))
    def _():
        wg_b[...] = wg_ref[...].astype(BF16)
        wu_b[...] = wu_ref[...].astype(BF16)
        xs = xs_ref[...]
        gs = _dot(xs, wg_b[...])
        acc_s = (cb_ref[...] + p0_ref[...] * cw_ref[0:1, :] + p1_ref[...] * cw_ref[1:2, :]
                 + gs * cw_ref[2:3, :])
        hs_ref[...] = (acc_s * _sigmoid(acc_s) * _dot(xs, wu_b[...])).astype(hs_ref.dtype)
        gs_ref[...] = gs

    @pl.when(t == 0)
    def _():
        carry[...] = prev_ref[...]

    x = x_ref[...]
    gt = _dot(x, wg_b[...])
    up = _dot(x, wu_b[...])
    rowi = lax.broadcasted_iota(jnp.int32, gt.shape, 0)
    c0, c1 = carry[0:1, :], carry[1:2, :]
    g1 = jnp.where(rowi == 0, c1, pltpu.roll(gt, 1, 0))
    g2 = jnp.where(rowi == 0, c0, jnp.where(rowi == 1, c1, pltpu.roll(gt, 2, 0)))
    acc = cb_ref[...] + g2 * cw_ref[0:1, :] + g1 * cw_ref[1:2, :] + gt * cw_ref[2:3, :]
    h_ref[...] = (acc * _sigmoid(acc) * up).astype(h_ref.dtype)
    carry[...] = gt[tm - 2:tm, :]

    @pl.when(t == n_t - 1)
    def _():
        cn_ref[...] = gt[tm - 2:tm, :]


def _ffn_in(xn, xn_s, w_gate, w_up, conv_w, conv_b, conv_prev, prev0_s, prev1_s):
    b, t, d = xn.shape
    ms = xn_s.shape[0]
    f = w_gate.shape[1]
    tm = _pick(t, (1024, 512, 256, 128, 64, 32, 16, 8))
    tn = _pick(f, (256, 128))
    assert tm >= CONV_W - 1
    n_t = t // tm
    wspec = pl.BlockSpec((d, tn), lambda j, bi, ti: (0, j))
    srows = pl.BlockSpec((ms, tn), lambda j, bi, ti: (0, j))
    return pl.pallas_call(
        functools.partial(_ffn_in_kernel, tm=tm, n_t=n_t),
        grid=(f // tn, b, n_t),
        in_specs=[pl.BlockSpec((None, tm, d), lambda j, bi, ti: (bi, ti, 0)),
                  pl.BlockSpec((ms, d), lambda j, bi, ti: (0, 0)),
                  wspec, wspec,
                  pl.BlockSpec((CONV_W, tn), lambda j, bi, ti: (0, j)),
                  pl.BlockSpec((1, tn), lambda j, bi, ti: (0, j)),
                  pl.BlockSpec((None, CONV_W - 1, tn), lambda j, bi, ti: (bi, 0, j)),
                  srows, srows],
        out_specs=[pl.BlockSpec((None, tm, tn), lambda j, bi, ti: (bi, ti, j)),
                   pl.BlockSpec((None, CONV_W - 1, tn), lambda j, bi, ti: (bi, 0, j)),
                   srows, srows],
        out_shape=[jax.ShapeDtypeStruct((b, t, f), BF16),
                   jax.ShapeDtypeStruct((b, CONV_W - 1, f), F32),
                   jax.ShapeDtypeStruct((ms, f), BF16),
                   jax.ShapeDtypeStruct((ms, f), F32)],
        scratch_shapes=[pltpu.VMEM((CONV_W - 1, tn), F32),
                        pltpu.VMEM((d, tn), BF16), pltpu.VMEM((d, tn), BF16)],
        compiler_params=_params(("parallel", "arbitrary", "arbitrary")),
        name="ffn_in",
    )(xn, xn_s, w_gate, w_up, conv_w, conv_b.reshape(1, f), conv_prev, prev0_s, prev1_s)


def _layout(d_model, rw_cols):
    rw = d_model // 2
    mb = d_model // 2
    lora = rw_cols - 3 * rw
    tn = next(c for c in (512, 256, LANE) if (3 * rw) % c == 0 and (-lora) % c <= LANE)
    shift = (-lora) % tn
    lp = lora + shift
    off_l = 3 * rw
    off_q = off_l + lp
    off_ga = off_q + 3 * mb
    return dict(rw=rw, mb=mb, lora=lora, lp=lp, tn=tn, shift=shift, off_l=off_l, off_q=off_q,
                off_mk=off_q + mb, off_mv=off_q + 2 * mb, off_ga=off_ga, off_gb=off_ga + d_model,
                np=off_ga + 2 * d_model)


def _split_prev(prev, lay):
    rw, lora, lp = lay["rw"], lay["lora"], lay["lp"]
    b = prev.shape[0]
    parts = [prev[:, i * rw:(i + 1) * rw] for i in range(3)]
    parts.append(jnp.pad(prev[:, 3 * rw:3 * rw + lora], ((0, 0), (0, lp - lora))))
    return [p.reshape(b, 1, -1) for p in parts]


def _unpermute_last(proj_last, lay):
    rw3, lora = 3 * lay["rw"], lay["lora"]
    return jnp.concatenate([proj_last[:, :rw3], proj_last[:, lay["off_l"]:lay["off_l"] + lora]], axis=-1)


def _rwkv_params(lay, rw_mu, rw_w0, rw_w2, rw_a0, rw_a2, rw_g2, rw_k_k, rw_k_a, rw_r_k, rw_ln_w, rw_ln_b):
    rw, lora, lp = lay["rw"], lay["lora"], lay["lp"]
    dw, da = rw_w2.shape[0], rw_a2.shape[0]
    row = lambda x: x.reshape(1, -1)

    def padded(w, start):
        return jnp.zeros((lp, rw), BF16).at[start:start + w.shape[0]].set(w.astype(BF16))

    return dict(
        mu_r=row(rw_mu[:rw]), mu_k=row(rw_mu[rw:2 * rw]), mu_v=row(rw_mu[2 * rw:3 * rw]),
        mu_l=row(jnp.pad(rw_mu[3 * rw:], (0, lp - lora))),
        w0=row(rw_w0), a0=row(rw_a0), k_k=row(rw_k_k), k_a=row(rw_k_a),
        w2p=padded(rw_w2, 0), a2p=padded(rw_a2, dw), g2p=padded(rw_g2, dw + da),
        r_k=row(rw_r_k), ln_w=row(rw_ln_w), ln_b=row(rw_ln_b))


def _mix_and_project(x2d, proj2d, o_a, o_b, w, lay):
    merged = _merge(o_a, o_b, w["w_a_out"], w["w_b_out"], proj2d, lay)
    return _matmul(merged, w["w_o"], res=x2d, tn=512)


def _after_mix(x1, mk, mv, b, t, w):
    xn = _rmsnorm(x1, w["norm_x"], BF16)
    q = _matmul(xn, w["wx_q"], out_dtype=BF16)
    o = _xattn(q.reshape(b, t, -1), mk, mv)
    return _matmul(o.reshape(b * t, -1), w["wx_o"], res=x1, tn=1024)


def kernel(x_prompt, x_sample, mem_prompt, cache_moba_k, cache_moba_v, page_table, cache_mem_k, cache_mem_v, state_rwkv_wkv, state_rwkv_shift, state_ffn_conv, norm_mix, w_in, rw_mu, rw_w0, rw_w2, rw_a0, rw_a2, rw_g2, rw_k_k, rw_k_a, rw_r_k, rw_ln_w, rw_ln_b, w_a_out, w_b_out, w_o, norm_x, norm_mem, wx_q, wx_k, wx_v, wx_o, norm_ffn, w_gate, w_up, conv_w, conv_b, w_down, norm_out):
    assert w_in.shape[0] == 1, "single-layer model"
    bp, t, d = x_prompt.shape
    db, ts, _ = x_sample.shape
    assert ts == 1, "decode group advances one token"
    rw_cols = state_rwkv_shift.shape[-1]
    lay = _layout(d, rw_cols)
    rw, mbw = lay["rw"], lay["mb"]
    nh_rw, nh_mb = rw // RW_HEAD, mbw // MB_HEAD
    f = w_gate.shape[-1]
    n_mem = mem_prompt.shape[1]
    xw = wx_q.shape[-1]
    n_pages = page_table.shape[1]
    page = cache_moba_k.shape[2]
    assert page == PAGE_SIZE and n_pages % (MB_BLOCK // page) == 0
    past_len = n_pages * page

    bf = lambda z: z[0].astype(BF16)
    w = dict(w_a_out=bf(w_a_out), w_b_out=bf(w_b_out), w_o=bf(w_o), norm_x=norm_x[0],
             wx_q=bf(wx_q), wx_o=bf(wx_o))
    prm = _rwkv_params(lay, rw_mu[0], rw_w0[0], rw_w2[0], rw_a0[0], rw_a2[0], rw_g2[0],
                       rw_k_k[0], rw_k_a[0], rw_r_k[0], rw_ln_w[0], rw_ln_b[0])
    w_down_b = bf(w_down)
    w_in_t = jnp.swapaxes(w_in[0], 0, 1)

    xp = x_prompt.reshape(bp * t, d)
    xs = x_sample.reshape(db, d)
    proj_p, proj_s = _project(_rmsnorm(xp, norm_mix[0], BF16), _rmsnorm(xs, norm_mix[0], BF16),
                              w_in_t, lay)
    proj_p3 = proj_p.reshape(bp, t, lay["np"])
    zeros_prev = [jnp.zeros((bp, 1, wd), F32) for wd in (rw, rw, rw, lay["lp"])]
    streams = _rwkv_prep(proj_p3, zeros_prev, lay, prm)
    o_a_p, sw_p, page_sums = _rwkv_scan(
        streams, jnp.zeros((bp, nh_rw, RW_HEAD, RW_HEAD), F32), prm, min(SCAN_CHUNK, t),
        paged_keys=(cache_moba_k[0], page_table.reshape(-1)))
    k_rot_p, v_p, o_b_p = _moba_prompt(proj_p3, lay)
    x1_p = _mix_and_project(xp, proj_p, o_a_p.reshape(bp * t, rw), o_b_p.reshape(bp * t, mbw), w, lay)

    streams_s = _rwkv_prep(proj_s.reshape(db, 1, lay["np"]), _split_prev(state_rwkv_shift[0], lay),
                           lay, prm)
    streams_s = [jnp.pad(s, ((0, 0), (0, STEP_CHUNK - 1), (0, 0))) for s in streams_s]
    o_a_s, sw_s = _rwkv_scan(streams_s, state_rwkv_wkv[0], prm, STEP_CHUNK)
    o_a_s = o_a_s[:, 0, :]
    q_s, k_rot_s = _rope_rows(proj_s, lay, past_len)
    v_s = proj_s[:, lay["off_mv"]:lay["off_mv"] + mbw]
    heads3 = lambda z: z.reshape(db, nh_mb, MB_HEAD)
    sel = _decode_select(heads3(q_s), page_sums.reshape(db, n_pages, nh_mb, MB_HEAD))
    sel = sel.reshape(db, MB_TOPK, nh_mb).transpose(0, 2, 1)
    o_b_s = _decode_attend(heads3(q_s), heads3(k_rot_s), heads3(v_s), sel, page_table,
                           cache_moba_k[0], cache_moba_v[0])
    x1_s = _mix_and_project(xs, proj_s, o_a_s, o_b_s.astype(BF16), w, lay)

    mem_n = _rmsnorm(mem_prompt.reshape(bp * n_mem, d), norm_mem[0], BF16)
    mk_p = _matmul(mem_n, wx_k[0])
    mv_p = _matmul(mem_n, wx_v[0])
    x2_p, xn3_p = _xblock(x1_p.reshape(bp, t, d), norm_x[0], w["wx_q"], w["wx_o"],
                          mk_p.reshape(bp, n_mem, xw), mv_p.reshape(bp, n_mem, xw), norm_ffn[0])
    x2_p = x2_p.reshape(bp * t, d)
    x2_s = _after_mix(x1_s, cache_mem_k[0].reshape(db, n_mem, xw), cache_mem_v[0].reshape(db, n_mem, xw),
                      db, 1, w)

    conv_prev_s = state_ffn_conv[0]
    hid_p, cv_p, hid_s, gt_s = _ffn_in(
        xn3_p, _rmsnorm(x2_s, norm_ffn[0], BF16), w_gate[0], w_up[0], conv_w[0], conv_b[0],
        jnp.zeros((bp, CONV_W - 1, f), F32), conv_prev_s[:, 0], conv_prev_s[:, 1])
    x3_p = _matmul(hid_p.reshape(bp * t, f), w_down_b, res=x2_p,
                   tm=_pick(bp * t, (512, 256, 128, 64, 32, 16, 8)))
    x3_s = _matmul(hid_s, w_down_b, res=x2_s)
    y_p = _rmsnorm(x3_p, norm_out, F32).reshape(bp, t, d)
    y_s = _rmsnorm(x3_s, norm_out, F32).reshape(db, 1, d)
    cv_s = jnp.stack([conv_prev_s[:, 1], gt_s], axis=1)

    return (y_p, y_s, k_rot_p[None], v_p[None],
            k_rot_s.reshape(1, db, 1, nh_mb, MB_HEAD), v_s.reshape(1, db, 1, nh_mb, MB_HEAD),
            mk_p.reshape(1, bp, n_mem, xw // X_HEAD, X_HEAD), mv_p.reshape(1, bp, n_mem, xw // X_HEAD, X_HEAD),
            sw_p[None], sw_s[None],
            _unpermute_last(proj_p3[:, t - 1], lay)[None], _unpermute_last(proj_s, lay)[None],
            cv_p[None], cv_s[None])
```

```python
import functools
import math

import jax
import jax.numpy as jnp
from jax import lax
from jax.experimental import pallas as pl
from jax.experimental.pallas import tpu as pltpu

F32 = jnp.float32
BF16 = jnp.bfloat16

NORM_EPS = 1e-5
RW_HEAD = 64
RW_GN_EPS = 64e-5
MB_HEAD = 128
MB_BLOCK = 256
MB_TOPK = 3
PAGE_SIZE = 128
ROPE_DIM = MB_HEAD // 4
ROPE_THETA = 500000.0
X_HEAD = 128
CONV_W = 3

LANE = 128
VMEM_LIMIT = 56 * 1024 * 1024
SCAN_CHUNK = 64
SCAN_HEADS = 16
STEP_CHUNK = 8


def _params(sem):
    return pltpu.CompilerParams(dimension_semantics=sem, vmem_limit_bytes=VMEM_LIMIT)


def _pick(n, prefs):
    for p in prefs:
        if n % p == 0:
            return p
    return n


def _dot(a, b, dims=(((1,), (0,)), ((), ())), exact=False):
    if exact:
        return lax.dot_general(a, b, dims, precision=lax.Precision.HIGHEST,
                               preferred_element_type=F32)
    return lax.dot_general(a.astype(BF16), b.astype(BF16), dims, preferred_element_type=F32)


_NT = (((1,), (1,)), ((), ()))
_TN = (((0,), (0,)), ((), ()))


def _sigmoid(x):
    return 1.0 / (1.0 + jnp.exp(-x))


def _rmsnorm_kernel(x_ref, g_ref, o_ref):
    x = x_ref[...]
    y = x * lax.rsqrt(jnp.mean(x * x, axis=-1, keepdims=True) + NORM_EPS)
    o_ref[...] = (y * g_ref[...]).astype(o_ref.dtype)


def _rmsnorm(x, g, out_dtype):
    m, d = x.shape
    tr = _pick(m, (256, 128, 64, 32, 16, 8))
    return pl.pallas_call(
        _rmsnorm_kernel,
        grid=(m // tr,),
        in_specs=[pl.BlockSpec((tr, d), lambda i: (i, 0)),
                  pl.BlockSpec((1, d), lambda i: (0, 0))],
        out_specs=pl.BlockSpec((tr, d), lambda i: (i, 0)),
        out_shape=jax.ShapeDtypeStruct((m, d), out_dtype),
        compiler_params=_params(("parallel",)),
        name="rmsnorm",
    )(x, g.reshape(1, d))


def _mm_kernel(a_ref, w_ref, *refs):
    o_ref = refs[-1]
    out = _dot(a_ref[...], w_ref[...])
    if len(refs) == 2:
        out = out + refs[0][...]
    o_ref[...] = out.astype(o_ref.dtype)


def _matmul(a, w, res=None, out_dtype=F32, tm=None, tn=256):
    m, kdim = a.shape
    n = w.shape[1]
    tm = tm or _pick(m, (1024, 512, 256, 128, 64, 32, 16, 8))
    tn = min(tn, n)
    assert m % tm == 0
    in_specs = [pl.BlockSpec((tm, kdim), lambda i, j: (i, 0)),
                pl.BlockSpec((kdim, tn), lambda i, j: (0, j))]
    args = [a, w]
    if res is not None:
        in_specs.append(pl.BlockSpec((tm, tn), lambda i, j: (i, j)))
        args.append(res)
    return pl.pallas_call(
        _mm_kernel,
        grid=(m // tm, pl.cdiv(n, tn)),
        in_specs=in_specs,
        out_specs=pl.BlockSpec((tm, tn), lambda i, j: (i, j)),
        out_shape=jax.ShapeDtypeStruct((m, n), out_dtype),
        compiler_params=_params(("parallel", "parallel")),
        name="matmul",
    )(*args)


def _proj_kernel(a_ref, as_ref, wt_ref, o_ref, os_ref, wb):
    @pl.when(pl.program_id(1) == 0)
    def _():
        wb[...] = wt_ref[...].astype(BF16)
        os_ref[...] = lax.dot_general(as_ref[...], wb[...], _NT, preferred_element_type=F32)

    o_ref[...] = lax.dot_general(a_ref[...], wb[...], _NT, preferred_element_type=F32)


def _project(xn, xn_s, w_in_t, lay):
    m, kdim = xn.shape
    ms = xn_s.shape[0]
    n_src = w_in_t.shape[0]
    tn, shift = lay["tn"], lay["shift"]
    first_shifted = lay["off_q"] // tn
    tm = _pick(m, (1024, 512, 256, 128, 64, 32, 16, 8))
    sub = 8
    assert lay["np"] % tn == 0 and lay["np"] == n_src + shift and shift % sub == 0
    return pl.pallas_call(
        _proj_kernel,
        grid=(lay["np"] // tn, m // tm),
        in_specs=[pl.BlockSpec((tm, kdim), lambda j, i: (i, 0)),
                  pl.BlockSpec((ms, kdim), lambda j, i: (0, 0)),
                  pl.BlockSpec((pl.Element(tn), pl.Element(kdim)),
                               lambda j, i: ((j * (tn // sub)
                                              - jnp.where(j >= first_shifted, shift // sub, 0)) * sub, 0))],
        out_specs=[pl.BlockSpec((tm, tn), lambda j, i: (i, j)),
                   pl.BlockSpec((ms, tn), lambda j, i: (0, j))],
        out_shape=[jax.ShapeDtypeStruct((m, lay["np"]), F32),
                   jax.ShapeDtypeStruct((ms, lay["np"]), F32)],
        scratch_shapes=[pltpu.VMEM((tn, kdim), BF16)],
        compiler_params=_params(("parallel", "arbitrary")),
        name="in_projection",
    )(xn, xn_s, w_in_t)


def _prep_kernel(zr_ref, zk_ref, zv_ref, zl_ref, pr_ref, pk_ref, pv_ref, pl_ref,
                 mur_ref, muk_ref, muv_ref, mul_ref, w0_ref, a0_ref, kk_ref, ka_ref,
                 w2_ref, a2_ref, g2_ref,
                 r_ref, lw_ref, k_ref, v_ref, kkr_ref, a_ref, g_ref,
                 cr, ck, cv, cl, *, tp):
    t = pl.program_id(1)

    @pl.when(t == 0)
    def _():
        cr[...] = pr_ref[...]
        ck[...] = pk_ref[...]
        cv[...] = pv_ref[...]
        cl[...] = pl_ref[...]

    def shift_mix(z_ref, carry, mu_ref):
        z = z_ref[...]
        if tp == 1:
            prev = carry[...]
        else:
            first = lax.broadcasted_iota(jnp.int32, z.shape, 0) == 0
            prev = jnp.where(first, carry[...], pltpu.roll(z, 1, 0))
        carry[...] = z[tp - 1:tp, :]
        return z + mu_ref[...] * (prev - z)

    r = shift_mix(zr_ref, cr, mur_ref)
    k = shift_mix(zk_ref, ck, muk_ref)
    v = shift_mix(zv_ref, cv, muv_ref)
    xl = shift_mix(zl_ref, cl, mul_ref)
    u = w0_ref[...] + _dot(jnp.tanh(xl), w2_ref[...])
    lw = -math.exp(-0.5) * _sigmoid(u)
    a = _sigmoid(a0_ref[...] + _dot(xl, a2_ref[...]))
    g = _dot(_sigmoid(xl), g2_ref[...])
    r_ref[...] = r
    lw_ref[...] = lw
    k_ref[...] = k * (1.0 + (a - 1.0) * ka_ref[...])
    v_ref[...] = v
    kkr_ref[...] = k * kk_ref[...]
    a_ref[...] = a
    g_ref[...] = g


def _rwkv_prep(proj, prev, lay, prm):
    b, t, _ = proj.shape
    rw, lp = lay["rw"], lay["lp"]
    tp = _pick(t, (128, 64, 32, 16, 8))
    assert lay["off_l"] % lp == 0
    lblk = lay["off_l"] // lp
    row = lambda j: pl.BlockSpec((None, tp, rw), lambda bi, ti, j=j: (bi, ti, j))
    prev_spec = lambda w: pl.BlockSpec((None, 1, w), lambda bi, ti: (bi, 0, 0))
    vec = lambda w: pl.BlockSpec((1, w), lambda bi, ti: (0, 0))
    mat = pl.BlockSpec((lp, rw), lambda bi, ti: (0, 0))
    out_spec = pl.BlockSpec((None, tp, rw), lambda bi, ti: (bi, ti, 0))
    out_sds = jax.ShapeDtypeStruct((b, t, rw), F32)
    return pl.pallas_call(
        functools.partial(_prep_kernel, tp=tp),
        grid=(b, t // tp),
        in_specs=[row(0), row(1), row(2),
                  pl.BlockSpec((None, tp, lp), lambda bi, ti: (bi, ti, lblk)),
                  prev_spec(rw), prev_spec(rw), prev_spec(rw), prev_spec(lp),
                  vec(rw), vec(rw), vec(rw), vec(lp), vec(rw), vec(rw), vec(rw), vec(rw),
                  mat, mat, mat],
        out_specs=[out_spec] * 7,
        out_shape=[out_sds] * 7,
        scratch_shapes=[pltpu.VMEM((1, rw), F32), pltpu.VMEM((1, rw), F32),
                        pltpu.VMEM((1, rw), F32), pltpu.VMEM((1, lp), F32)],
        compiler_params=_params(("parallel", "arbitrary")),
        name="rwkv_prep",
    )(proj, proj, proj, proj, *prev,
      prm["mu_r"], prm["mu_k"], prm["mu_v"], prm["mu_l"], prm["w0"], prm["a0"],
      prm["k_k"], prm["k_a"], prm["w2p"], prm["a2p"], prm["g2p"])


def _scan_kernel(*refs, chunk, heads, n_chunks, n_side):
    if n_side:
        refs = refs[1:]
    (r_ref, lw_ref, k_ref, v_ref, kkr_ref, a_ref, g_ref, rk_ref, lnw_ref, lnb_ref, s0_ref) = refs[:11]
    page_refs = refs[11:11 + n_side]
    o_ref, sf_ref = refs[11 + n_side:13 + n_side]
    if n_side:
        side_ref = refs[13 + n_side]
        for s in range(n_side):
            side_ref[s] = jnp.sum(page_refs[s][...], axis=0)
    s_scr = refs[-1]
    c = pl.program_id(2)

    @pl.when(c == 0)
    def _():
        s_scr[...] = s0_ref[...]

    n = chunk
    row = lax.broadcasted_iota(jnp.int32, (n, n), 0)
    col = lax.broadcasted_iota(jnp.int32, (n, n), 1)
    incl = row >= col
    strict = row > col
    eye = (row == col).astype(F32)

    lw = lw_ref[...]
    cum = _dot(incl.astype(F32), lw, exact=True)
    cum_prev = cum - lw
    total = cum[n - 1:n, :]
    e_cum = jnp.exp(cum)
    e_prev = jnp.exp(cum_prev)
    e_neg = jnp.exp(-cum)
    e_rest = jnp.exp(total - cum)
    e_total = jnp.exp(total)

    r_all, k_all, v_all = r_ref[...], k_ref[...], v_ref[...]
    kkr_all, a_all, g_all = kkr_ref[...], a_ref[...], g_ref[...]
    rk_all, lnw_all, lnb_all = rk_ref[...], lnw_ref[...], lnb_ref[...]

    hs = range(heads)
    sls = [slice(h * RW_HEAD, (h + 1) * RW_HEAD) for h in hs]
    r = [r_all[:, sl] for sl in sls]
    k = [k_all[:, sl] for sl in sls]
    v = [v_all[:, sl] for sl in sls]
    kk, bb, q2 = [], [], []
    for h, sl in enumerate(sls):
        kkr = kkr_all[:, sl]
        kk_h = kkr / jnp.maximum(jnp.sqrt(jnp.sum(kkr * kkr, axis=-1, keepdims=True)), 1e-12)
        kk.append(kk_h)
        bb.append(kk_h * a_all[:, sl])
        q2.append(jnp.concatenate([kk_h * e_prev[:, sl], r[h] * e_cum[:, sl]], axis=0))
    p_k = [_dot(q2[h], k[h] * e_neg[:, sls[h]], _NT) for h in hs]
    p_b = [_dot(q2[h], bb[h] * e_neg[:, sls[h]], _NT) for h in hs]
    s = [s_scr[h] for h in hs]
    qs = [_dot(q2[h], s[h], _NT) for h in hs]
    m_k = [jnp.where(strict, p[:n], 0.0) for p in p_k]
    a_k = [jnp.where(incl, p[n:], 0.0) for p in p_k]
    a_b = [jnp.where(incl, p[n:], 0.0) for p in p_b]
    npow = [jnp.where(strict, -p[:n], 0.0) for p in p_b]
    tinv = [eye + x for x in npow]
    for _ in range(int(math.log2(n)) - 1):
        npow = [_dot(x, x) for x in npow]
        tinv = [t + _dot(t, x) for t, x in zip(tinv, npow)]
    rhs = [qs[h][:n] + _dot(m_k[h], v[h]) for h in hs]
    u = [_dot(tinv[h], rhs[h]) for h in hs]
    y = [qs[h][n:] + _dot(a_k[h], v[h]) - _dot(a_b[h], u[h]) for h in hs]
    for h, sl in enumerate(sls):
        s_scr[h] = (s[h] * e_total[:, sl] + _dot(v[h], k[h] * e_rest[:, sl], _TN)
                    - _dot(u[h], bb[h] * e_rest[:, sl], _TN))
    outs = []
    for h, sl in enumerate(sls):
        mu = jnp.mean(y[h], axis=-1, keepdims=True)
        yc = y[h] - mu
        var = jnp.mean(yc * yc, axis=-1, keepdims=True)
        yn = yc * lax.rsqrt(var + RW_GN_EPS) * lnw_all[:, sl] + lnb_all[:, sl]
        bonus = jnp.sum(r[h] * k[h] * rk_all[:, sl], axis=-1, keepdims=True) * v[h]
        outs.append((yn + bonus) * g_all[:, sl])
    o_ref[...] = jnp.concatenate(outs, axis=1).astype(o_ref.dtype)

    @pl.when(c == n_chunks - 1)
    def _():
        sf_ref[...] = s_scr[...]


def _rwkv_scan(streams, s0, prm, chunk, paged_keys=None):
    b, t, rw = streams[0].shape
    nh = rw // RW_HEAD
    hg = _pick(nh, (SCAN_HEADS, 4, 2))
    w = hg * RW_HEAD
    assert t % chunk == 0 and chunk & (chunk - 1) == 0 and w % LANE == 0
    n_chunks, n_groups = t // chunk, nh // hg
    stream_spec = pl.BlockSpec((None, chunk, w), lambda bi, gi, ci, *_: (bi, ci, gi))
    vec_spec = pl.BlockSpec((1, w), lambda bi, gi, ci, *_: (0, gi))
    state_spec = pl.BlockSpec((None, hg, RW_HEAD, RW_HEAD), lambda bi, gi, ci, *_: (bi, gi, 0, 0))
    in_specs = [stream_spec] * 7 + [vec_spec] * 3 + [state_spec]
    out_specs = [stream_spec, state_spec]
    out_shape = [jax.ShapeDtypeStruct((b, t, rw), BF16),
                 jax.ShapeDtypeStruct((b, nh, RW_HEAD, RW_HEAD), F32)]
    args = [*streams, prm["r_k"], prm["ln_w"], prm["ln_b"], s0]
    n_side, n_prefetch = 0, 0
    if paged_keys is not None:
        cache_k, page_ids = paged_keys
        _, page, nhk, d = cache_k.shape
        n_ids = page_ids.shape[0]
        n_steps = b * n_groups * n_chunks
        n_side = pl.cdiv(n_ids, n_steps)
        step = lambda bi, gi, ci: (bi * n_groups + gi) * n_chunks + ci

        def page_spec(s):
            return pl.BlockSpec(
                (None, page, nhk, d),
                lambda bi, gi, ci, ids: (ids[jnp.minimum(step(bi, gi, ci) * n_side + s, n_ids - 1)], 0, 0, 0))

        in_specs += [page_spec(s) for s in range(n_side)]
        out_specs.append(pl.BlockSpec((n_side, nhk, d), lambda bi, gi, ci, ids: (step(bi, gi, ci), 0, 0)))
        out_shape.append(jax.ShapeDtypeStruct((n_steps * n_side, nhk, d), F32))
        args = [page_ids] + args + [cache_k] * n_side
        n_prefetch = 1
    outs = pl.pallas_call(
        functools.partial(_scan_kernel, chunk=chunk, heads=hg, n_chunks=n_chunks, n_side=n_side),
        grid_spec=pltpu.PrefetchScalarGridSpec(
            num_scalar_prefetch=n_prefetch,
            grid=(b, n_groups, n_chunks),
            in_specs=in_specs,
            out_specs=out_specs,
            scratch_shapes=[pltpu.VMEM((hg, RW_HEAD, RW_HEAD), F32)],
        ),
        out_shape=out_shape,
        compiler_params=_params(("parallel", "parallel", "arbitrary")),
        name="rwkv_scan",
    )(*args)
    if paged_keys is not None:
        return outs[0], outs[1], outs[2][:n_ids]
    return outs[0], outs[1]


def _rope_tables(pos):
    half = ROPE_DIM // 2
    inv = 1.0 / (ROPE_THETA ** (jnp.arange(0, ROPE_DIM, 2, dtype=F32) / ROPE_DIM))
    ang = pos.astype(F32)[:, None] * inv[None, :]
    cos, sin = jnp.cos(ang), jnp.sin(ang)
    t = pos.shape[0]
    rest = MB_HEAD - ROPE_DIM
    c = jnp.concatenate([cos, cos, jnp.ones((t, rest), F32)], axis=1)
    s = jnp.concatenate([-sin, sin, jnp.zeros((t, rest), F32)], axis=1)
    del half
    return c, s


def _rope(x, c, s):
    half = ROPE_DIM // 2
    lane = lax.broadcasted_iota(jnp.int32, x.shape, 1)
    swapped = jnp.where(lane < half, pltpu.roll(x, MB_HEAD - half, 1), pltpu.roll(x, half, 1))
    return x * c + swapped * s


def _moba_prompt_kernel(q_ref, k_ref, v_ref, c_ref, s_ref, hot_ref, kout_hbm, vout_hbm, o_ref,
                        kbuf, vbuf, sems, *, nb):
    bi, hi = pl.program_id(0), pl.program_id(1)
    c, s = c_ref[...], s_ref[...]
    q = _rope(q_ref[...], c, s)
    k = _rope(k_ref[...], c, s)
    kbuf[...] = k
    vbuf[...] = v_ref[...]
    k_copy = pltpu.make_async_copy(kbuf, kout_hbm.at[bi, :, hi, :], sems.at[0])
    v_copy = pltpu.make_async_copy(vbuf, vout_hbm.at[bi, :, hi, :], sems.at[1])
    k_copy.start()
    v_copy.start()
    t = q.shape[0]
    k_means = jnp.sum(k.reshape(nb, MB_BLOCK, MB_HEAD), axis=1) * (1.0 / MB_BLOCK)
    nbp = -(-nb // 8) * 8
    if nbp > nb:
        k_means = jnp.concatenate([k_means, jnp.zeros((nbp - nb, MB_HEAD), F32)], axis=0)
    scores = _dot(k_means, q, _NT, exact=True)
    blk = lax.broadcasted_iota(jnp.int32, (nbp, t), 0)
    tpos = lax.broadcasted_iota(jnp.int32, (nbp, t), 1)
    rank = jnp.zeros((nbp, t), jnp.int32)
    for m in range(nb):
        sm = scores[m:m + 1, :]
        ahead = (sm > scores) | ((sm == scores) & (blk > m))
        rank = rank + jnp.where(ahead & (tpos >= (m + 1) * MB_BLOCK), 1, 0)
    own = (tpos >= blk * MB_BLOCK) & (tpos < (blk + 1) * MB_BLOCK)
    sel = ((rank < MB_TOPK) & (tpos >= (blk + 1) * MB_BLOCK)) | own
    eye = (lax.broadcasted_iota(jnp.int32, (nbp, MB_HEAD), 0)
           == lax.broadcasted_iota(jnp.int32, (nbp, MB_HEAD), 1)).astype(F32)
    sel_rows = _dot(jnp.where(sel, 1.0, 0.0), eye, _TN)
    lane = lax.broadcasted_iota(jnp.int32, (t, MB_HEAD), 1)
    masked = -1e30
    bias = jnp.where((lane < nb) & (sel_rows < 0.5), masked, 0.0)
    scale = MB_HEAD ** -0.5 * math.log2(math.e)
    qa = jnp.concatenate([(q * scale).astype(BF16), bias.astype(BF16)], axis=1)
    ka = jnp.concatenate([k.astype(BF16), hot_ref[...]], axis=1)
    vb = v_ref[...].astype(BF16)
    rowi = lax.broadcasted_iota(jnp.int32, (MB_BLOCK, MB_BLOCK), 0)
    coli = lax.broadcasted_iota(jnp.int32, (MB_BLOCK, MB_BLOCK), 1)
    causal = rowi >= coli
    for i in range(nb):
        rows = slice(i * MB_BLOCK, (i + 1) * MB_BLOCK)
        q_i = qa[rows]
        logits = [_dot(q_i, ka[j * MB_BLOCK:(j + 1) * MB_BLOCK], _NT) for j in range(i)]
        logits.append(jnp.where(causal, _dot(q_i, ka[rows], _NT), masked))
        top = logits[0]
        for lg in logits[1:]:
            top = jnp.maximum(top, lg)
        mx = top.max(axis=-1, keepdims=True)
        p_sum = jnp.zeros((MB_BLOCK, MB_BLOCK), F32)
        acc = jnp.zeros((MB_BLOCK, MB_HEAD), F32)
        for j, lg in enumerate(logits):
            pj = jnp.exp2(lg - mx)
            p_sum = p_sum + pj
            acc = acc + _dot(pj, vb[j * MB_BLOCK:(j + 1) * MB_BLOCK])
        o_ref[rows, :] = (acc / jnp.sum(p_sum, axis=-1, keepdims=True)).astype(o_ref.dtype)
    k_copy.wait()
    v_copy.wait()


def _moba_prompt(proj, lay):
    b, t, _ = proj.shape
    assert t % MB_BLOCK == 0
    mbw = lay["mb"]
    nh = mbw // MB_HEAD
    assert t // MB_BLOCK <= MB_HEAD
    ctab, stab = _rope_tables(jnp.arange(t))
    block_hot = (jnp.arange(t)[:, None] // MB_BLOCK == jnp.arange(MB_HEAD)[None, :]).astype(BF16)
    qb, kb, vb = lay["off_q"] // MB_HEAD, lay["off_mk"] // MB_HEAD, lay["off_mv"] // MB_HEAD
    head = lambda off: pl.BlockSpec((None, t, MB_HEAD), lambda bi, hi, off=off: (bi, 0, off + hi))
    tab = pl.BlockSpec((t, MB_HEAD), lambda bi, hi: (0, 0))
    hbm = pl.BlockSpec(memory_space=pl.ANY)
    rows_sds = jax.ShapeDtypeStruct((b, t, nh, MB_HEAD), F32)
    return pl.pallas_call(
        functools.partial(_moba_prompt_kernel, nb=t // MB_BLOCK),
        grid=(b, nh),
        in_specs=[head(qb), head(kb), head(vb), tab, tab, tab],
        out_specs=[hbm, hbm, pl.BlockSpec((None, t, MB_HEAD), lambda bi, hi: (bi, 0, hi))],
        out_shape=[rows_sds, rows_sds, jax.ShapeDtypeStruct((b, t, mbw), BF16)],
        scratch_shapes=[pltpu.VMEM((t, MB_HEAD), F32), pltpu.VMEM((t, MB_HEAD), F32),
                        pltpu.SemaphoreType.DMA((2,))],
        compiler_params=_params(("parallel", "parallel")),
        name="moba_prompt",
    )(proj, proj, proj, ctab, stab, block_hot)


def _rope_rows_kernel(q_ref, k_ref, c_ref, s_ref, qo_ref, ko_ref, *, heads):
    c, s = c_ref[...], s_ref[...]
    for h in range(heads):
        sl = slice(h * MB_HEAD, (h + 1) * MB_HEAD)
        qo_ref[:, sl] = _rope(q_ref[:, sl], c, s)
        ko_ref[:, sl] = _rope(k_ref[:, sl], c, s)


def _rope_rows(proj2d, lay, pos):
    m = proj2d.shape[0]
    mbw = lay["mb"]
    ctab, stab = _rope_tables(jnp.full((1,), pos))
    cw = math.gcd(math.gcd(lay["off_q"], lay["off_mk"]), mbw)
    assert cw % MB_HEAD == 0
    spec = lambda off: pl.BlockSpec((m, cw), lambda i, off=off: (0, off // cw + i))
    tab = pl.BlockSpec((1, MB_HEAD), lambda i: (0, 0))
    out = pl.BlockSpec((m, cw), lambda i: (0, i))
    sds = jax.ShapeDtypeStruct((m, mbw), F32)
    return pl.pallas_call(
        functools.partial(_rope_rows_kernel, heads=cw // MB_HEAD),
        grid=(mbw // cw,),
        in_specs=[spec(lay["off_q"]), spec(lay["off_mk"]), tab, tab],
        out_specs=[out, out],
        out_shape=[sds, sds],
        compiler_params=_params(("arbitrary",)),
        name="rope_rows",
    )(proj2d, proj2d, ctab, stab)


def _decode_select_kernel(q_ref, ps_ref, idx_ref, *, nb, ppb):
    q = q_ref[...]
    page_sums = ps_ref[...]
    block_sums = jnp.sum(page_sums.reshape((nb, ppb) + page_sums.shape[1:]), axis=1)
    scores = jnp.sum(block_sums * (1.0 / MB_BLOCK) * q[None], axis=-1, keepdims=True)
    blk = lax.broadcasted_iota(jnp.int32, scores.shape, 0)
    rank = jnp.zeros(scores.shape, jnp.int32)
    for m in range(nb):
        sm = scores[m:m + 1]
        ahead = (sm > scores) | ((sm == scores) & (blk > m))
        rank = rank + ahead.astype(jnp.int32)
    for r in range(MB_TOPK):
        idx_ref[r] = jnp.sum(jnp.where(rank == r, blk, 0), axis=0)


def _decode_select(q, page_sums):
    db, nh, d = q.shape
    n_pages = page_sums.shape[1]
    ppb = MB_BLOCK // PAGE_SIZE
    nb = n_pages // ppb
    assert nb >= MB_TOPK
    return pl.pallas_call(
        functools.partial(_decode_select_kernel, nb=nb, ppb=ppb),
        grid=(db,),
        in_specs=[pl.BlockSpec((None, nh, d), lambda bi: (bi, 0, 0)),
                  pl.BlockSpec((None, n_pages, nh, d), lambda bi: (bi, 0, 0, 0))],
        out_specs=pl.BlockSpec((None, MB_TOPK, nh, 1), lambda bi: (bi, 0, 0, 0)),
        out_shape=jax.ShapeDtypeStruct((db, MB_TOPK, nh, 1), jnp.int32),
        compiler_params=_params(("parallel",)),
        name="moba_decode_select",
    )(q, page_sums)


def _decode_attend_kernel(sel_ref, pt_ref, q_ref, kn_ref, vn_ref, ck_hbm, cv_hbm, o_ref,
                          kbuf, vbuf, sems, *, nh, ppb):
    bi = pl.program_id(0)
    n_sel = MB_TOPK * ppb
    page = kbuf.shape[2]

    def head_copies(h):
        copies = []
        for r in range(MB_TOPK):
            blk = sel_ref[(bi * nh + h) * MB_TOPK + r]
            for j in range(ppb):
                phys = pt_ref[bi, blk * ppb + j]
                s = r * ppb + j
                copies.append(pltpu.make_async_copy(ck_hbm.at[phys, :, h, :], kbuf.at[h, s], sems.at[h]))
                copies.append(pltpu.make_async_copy(cv_hbm.at[phys, :, h, :], vbuf.at[h, s], sems.at[h]))
        return copies

    all_copies = [head_copies(h) for h in range(nh)]
    for copies in all_copies:
        for cp in copies:
            cp.start()

    scale = MB_HEAD ** -0.5
    for h in range(nh):
        for cp in all_copies[h]:
            cp.wait()
        q = q_ref[h:h + 1, :]
        keys = kbuf[h].reshape(n_sel * page, MB_HEAD)
        vals = vbuf[h].reshape(n_sel * page, MB_HEAD)
        lg = _dot(jnp.broadcast_to(q, (8, MB_HEAD)), keys, _NT)[0:1] * scale
        own = jnp.sum(q * kn_ref[h:h + 1, :], axis=-1, keepdims=True) * scale
        mx = jnp.maximum(own, lg.max(axis=-1, keepdims=True))
        p_own = jnp.exp(own - mx)
        p = jnp.exp(lg - mx)
        den = p_own + jnp.sum(p, axis=-1, keepdims=True)
        acc = p_own * vn_ref[h:h + 1, :] + _dot(jnp.broadcast_to(p, (8, p.shape[1])), vals)[0:1]
        o_ref[h:h + 1, :] = acc / den


def _decode_attend(q, k_new, v_new, sel, page_table, cache_k, cache_v):
    db, nh, d = q.shape
    page = cache_k.shape[1]
    ppb = MB_BLOCK // page
    n_sel = MB_TOPK * ppb
    vec = pl.BlockSpec((None, nh, d), lambda bi, sel_r, pt_r: (bi, 0, 0))
    hbm = pl.BlockSpec(memory_space=pl.ANY)
    out = pl.pallas_call(
        functools.partial(_decode_attend_kernel, nh=nh, ppb=ppb),
        grid_spec=pltpu.PrefetchScalarGridSpec(
            num_scalar_prefetch=2,
            grid=(db,),
            in_specs=[vec, vec, vec, hbm, hbm],
            out_specs=vec,
            scratch_shapes=[pltpu.VMEM((nh, n_sel, page, d), F32),
                            pltpu.VMEM((nh, n_sel, page, d), F32),
                            pltpu.SemaphoreType.DMA((nh,))],
        ),
        out_shape=jax.ShapeDtypeStruct((db, nh, d), F32),
        compiler_params=_params(("arbitrary",)),
        name="moba_decode_attend",
    )(sel.reshape(-1), page_table, q, k_new, v_new, cache_k, cache_v)
    return out.reshape(db, nh * d)


def _merge_kernel(oa_ref, ob_ref, wa_ref, wb_ref, ga_ref, gb_ref, o_ref):
    ya = _dot(oa_ref[...], wa_ref[...])
    yb = _dot(ob_ref[...], wb_ref[...])
    o_ref[...] = (_sigmoid(ga_ref[...]) * ya + _sigmoid(gb_ref[...]) * yb).astype(o_ref.dtype)


def _merge(o_a, o_b, w_a, w_b, proj2d, lay):
    m, ka = o_a.shape
    kb = o_b.shape[1]
    d = w_a.shape[1]
    tm = _pick(m, (1024, 512, 256, 128, 64, 32, 16, 8))
    tn = next(c for c in (512, 256, LANE)
              if d % c == 0 and lay["off_ga"] % c == 0 and lay["off_gb"] % c == 0)
    ga, gb = lay["off_ga"] // tn, lay["off_gb"] // tn
    return pl.pallas_call(
        _merge_kernel,
        grid=(m // tm, d // tn),
        in_specs=[pl.BlockSpec((tm, ka), lambda i, j: (i, 0)),
                  pl.BlockSpec((tm, kb), lambda i, j: (i, 0)),
                  pl.BlockSpec((ka, tn), lambda i, j: (0, j)),
                  pl.BlockSpec((kb, tn), lambda i, j: (0, j)),
                  pl.BlockSpec((tm, tn), lambda i, j: (i, ga + j)),
                  pl.BlockSpec((tm, tn), lambda i, j: (i, gb + j))],
        out_specs=pl.BlockSpec((tm, tn), lambda i, j: (i, j)),
        out_shape=jax.ShapeDtypeStruct((m, d), BF16),
        compiler_params=_params(("parallel", "parallel")),
        name="gated_merge",
    )(o_a, o_b, w_a, w_b, proj2d, proj2d)


def _xattn_kernel(q_ref, k_ref, v_ref, o_ref, *, heads):
    scale = X_HEAD ** -0.5
    for h in range(heads):
        sl = slice(h * X_HEAD, (h + 1) * X_HEAD)
        q = q_ref[:, sl]
        if q.shape[0] < 8:
            q = jnp.broadcast_to(q, (8, X_HEAD))
        lg = _dot(q, k_ref[:, sl], _NT) * scale
        p = jnp.exp(lg - lg.max(axis=-1, keepdims=True))
        o = _dot(p, v_ref[:, sl]) / jnp.sum(p, axis=-1, keepdims=True)
        o_ref[:, sl] = o[:o_ref.shape[0]].astype(o_ref.dtype)


def _xattn(q, mk, mv):
    b, t, xw = q.shape
    n_mem = mk.shape[1]
    tq = _pick(t, (512, 256, 128, 64, 32, 16, 8))
    qspec = pl.BlockSpec((None, tq, xw), lambda bi, ti: (bi, ti, 0))
    mspec = pl.BlockSpec((None, n_mem, xw), lambda bi, ti: (bi, 0, 0))
    return pl.pallas_call(
        functools.partial(_xattn_kernel, heads=xw // X_HEAD),
        grid=(b, t // tq),
        in_specs=[qspec, mspec, mspec],
        out_specs=qspec,
        out_shape=jax.ShapeDtypeStruct((b, t, xw), BF16),
        compiler_params=_params(("parallel", "parallel")),
        name="cross_attention",
    )(q, mk, mv)


def _xblock_kernel(x_ref, gx_ref, wq_ref, wo_ref, k_ref, v_ref, gf_ref, x2_ref, xn_ref, *, heads):
    x = x_ref[...]
    xn = x * lax.rsqrt(jnp.mean(x * x, axis=-1, keepdims=True) + NORM_EPS) * gx_ref[...]
    q = _dot(xn, wq_ref[...])
    scale = X_HEAD ** -0.5
    outs = []
    for h in range(heads):
        sl = slice(h * X_HEAD, (h + 1) * X_HEAD)
        lg = _dot(q[:, sl], k_ref[:, sl], _NT) * scale
        p = jnp.exp(lg - lg.max(axis=-1, keepdims=True))
        outs.append(_dot(p, v_ref[:, sl]) / jnp.sum(p, axis=-1, keepdims=True))
    x2 = x + _dot(jnp.concatenate(outs, axis=1), wo_ref[...])
    x2_ref[...] = x2
    xn_ref[...] = (x2 * lax.rsqrt(jnp.mean(x2 * x2, axis=-1, keepdims=True) + NORM_EPS)
                   * gf_ref[...]).astype(xn_ref.dtype)


def _xblock(x, g_x, wq, wo, mk, mv, g_next):
    b, t, d = x.shape
    xw = wq.shape[1]
    n_mem = mk.shape[1]
    tq = _pick(t, (256, 128, 64, 32, 16, 8))
    rows = pl.BlockSpec((None, tq, d), lambda bi, ti: (bi, ti, 0))
    vec = pl.BlockSpec((1, d), lambda bi, ti: (0, 0))
    mem = pl.BlockSpec((None, n_mem, xw), lambda bi, ti: (bi, 0, 0))
    return pl.pallas_call(
        functools.partial(_xblock_kernel, heads=xw // X_HEAD),
        grid=(b, t // tq),
        in_specs=[rows, vec, pl.BlockSpec((d, xw), lambda bi, ti: (0, 0)),
                  pl.BlockSpec((xw, d), lambda bi, ti: (0, 0)), mem, mem, vec],
        out_specs=[rows, rows],
        out_shape=[jax.ShapeDtypeStruct((b, t, d), F32), jax.ShapeDtypeStruct((b, t, d), BF16)],
        compiler_params=_params(("parallel", "parallel")),
        name="cross_attention_block",
    )(x, g_x.reshape(1, d), wq, wo, mk, mv, g_next.reshape(1, d))


def _ffn_in_kernel(x_ref, xs_ref, wg_ref, wu_ref, cw_ref, cb_ref, prev_ref, p0_ref, p1_ref,
                   h_ref, cn_ref, hs_ref, gs_ref, carry, wg_b, wu_b, *, tm, n_t):
    t = pl.program_id(2)

    @pl.when((pl.program_id(1) == 0) & (t == 0))
    def _():
        wg_b[...] = wg_ref[...].astype(BF16)
        wu_b[...] = wu_ref[...].astype(BF16)
        xs = xs_ref[...]
        gs = _dot(xs, wg_b[...])
        acc_s = (cb_ref[...] + p0_ref[...] * cw_ref[0:1, :] + p1_ref[...] * cw_ref[1:2, :]
                 + gs * cw_ref[2:3, :])
        hs_ref[...] = (acc_s * _sigmoid(acc_s) * _dot(xs, wu_b[...])).astype(hs_ref.dtype)
        gs_ref[...] = gs

    @pl.when(t == 0)
    def _():
        carry[...] = prev_ref[...]

    x = x_ref[...]
    gt = _dot(x, wg_b[...])
    up = _dot(x, wu_b[...])
    rowi = lax.broadcasted_iota(jnp.int32, gt.shape, 0)
    c0, c1 = carry[0:1, :], carry[1:2, :]
    g1 = jnp.where(rowi == 0, c1, pltpu.roll(gt, 1, 0))
    g2 = jnp.where(rowi == 0, c0, jnp.where(rowi == 1, c1, pltpu.roll(gt, 2, 0)))
    acc = cb_ref[...] + g2 * cw_ref[0:1, :] + g1 * cw_ref[1:2, :] + gt * cw_ref[2:3, :]
    h_ref[...] = (acc * _sigmoid(acc) * up).astype(h_ref.dtype)
    carry[...] = gt[tm - 2:tm, :]

    @pl.when(t == n_t - 1)
    def _():
        cn_ref[...] = gt[tm - 2:tm, :]


def _ffn_in(xn, xn_s, w_gate, w_up, conv_w, conv_b, conv_prev, prev0_s, prev1_s):
    b, t, d = xn.shape
    ms = xn_s.shape[0]
    f = w_gate.shape[1]
    tm = _pick(t, (1024, 512, 256, 128, 64, 32, 16, 8))
    tn = _pick(f, (256, 128))
    assert tm >= CONV_W - 1
    n_t = t // tm
    wspec = pl.BlockSpec((d, tn), lambda j, bi, ti: (0, j))
    srows = pl.BlockSpec((ms, tn), lambda j, bi, ti: (0, j))
    return pl.pallas_call(
        functools.partial(_ffn_in_kernel, tm=tm, n_t=n_t),
        grid=(f // tn, b, n_t),
        in_specs=[pl.BlockSpec((None, tm, d), lambda j, bi, ti: (bi, ti, 0)),
                  pl.BlockSpec((ms, d), lambda j, bi, ti: (0, 0)),
                  wspec, wspec,
                  pl.BlockSpec((CONV_W, tn), lambda j, bi, ti: (0, j)),
                  pl.BlockSpec((1, tn), lambda j, bi, ti: (0, j)),
                  pl.BlockSpec((None, CONV_W - 1, tn), lambda j, bi, ti: (bi, 0, j)),
                  srows, srows],
        out_specs=[pl.BlockSpec((None, tm, tn), lambda j, bi, ti: (bi, ti, j)),
                   pl.BlockSpec((None, CONV_W - 1, tn), lambda j, bi, ti: (bi, 0, j)),
                   srows, srows],
        out_shape=[jax.ShapeDtypeStruct((b, t, f), BF16),
                   jax.ShapeDtypeStruct((b, CONV_W - 1, f), F32),
                   jax.ShapeDtypeStruct((ms, f), BF16),
                   jax.ShapeDtypeStruct((ms, f), F32)],
        scratch_shapes=[pltpu.VMEM((CONV_W - 1, tn), F32),
                        pltpu.VMEM((d, tn), BF16), pltpu.VMEM((d, tn), BF16)],
        compiler_params=_params(("parallel", "arbitrary", "arbitrary")),
        name="ffn_in",
    )(xn, xn_s, w_gate, w_up, conv_w, conv_b.reshape(1, f), conv_prev, prev0_s, prev1_s)


def _layout(d_model, rw_cols):
    rw = d_model // 2
    mb = d_model // 2
    lora = rw_cols - 3 * rw
    tn = next(c for c in (512, 256, LANE) if (3 * rw) % c == 0 and (-lora) % c <= LANE)
    shift = (-lora) % tn
    lp = lora + shift
    off_l = 3 * rw
    off_q = off_l + lp
    off_ga = off_q + 3 * mb
    return dict(rw=rw, mb=mb, lora=lora, lp=lp, tn=tn, shift=shift, off_l=off_l, off_q=off_q,
                off_mk=off_q + mb, off_mv=off_q + 2 * mb, off_ga=off_ga, off_gb=off_ga + d_model,
                np=off_ga + 2 * d_model)


def _split_prev(prev, lay):
    rw, lora, lp = lay["rw"], lay["lora"], lay["lp"]
    b = prev.shape[0]
    parts = [prev[:, i * rw:(i + 1) * rw] for i in range(3)]
    parts.append(jnp.pad(prev[:, 3 * rw:3 * rw + lora], ((0, 0), (0, lp - lora))))
    return [p.reshape(b, 1, -1) for p in parts]


def _unpermute_last(proj_last, lay):
    rw3, lora = 3 * lay["rw"], lay["lora"]
    return jnp.concatenate([proj_last[:, :rw3], proj_last[:, lay["off_l"]:lay["off_l"] + lora]], axis=-1)


def _rwkv_params(lay, rw_mu, rw_w0, rw_w2, rw_a0, rw_a2, rw_g2, rw_k_k, rw_k_a, rw_r_k, rw_ln_w, rw_ln_b):
    rw, lora, lp = lay["rw"], lay["lora"], lay["lp"]
    dw, da = rw_w2.shape[0], rw_a2.shape[0]
    row = lambda x: x.reshape(1, -1)

    def padded(w, start):
        return jnp.zeros((lp, rw), BF16).at[start:start + w.shape[0]].set(w.astype(BF16))

    return dict(
        mu_r=row(rw_mu[:rw]), mu_k=row(rw_mu[rw:2 * rw]), mu_v=row(rw_mu[2 * rw:3 * rw]),
        mu_l=row(jnp.pad(rw_mu[3 * rw:], (0, lp - lora))),
        w0=row(rw_w0), a0=row(rw_a0), k_k=row(rw_k_k), k_a=row(rw_k_a),
        w2p=padded(rw_w2, 0), a2p=padded(rw_a2, dw), g2p=padded(rw_g2, dw + da),
        r_k=row(rw_r_k), ln_w=row(rw_ln_w), ln_b=row(rw_ln_b))


def _mix_and_project(x2d, proj2d, o_a, o_b, w, lay):
    merged = _merge(o_a, o_b, w["w_a_out"], w["w_b_out"], proj2d, lay)
    return _matmul(merged, w["w_o"], res=x2d, tn=1024)


def _after_mix(x1, mk, mv, b, t, w):
    xn = _rmsnorm(x1, w["norm_x"], BF16)
    q = _matmul(xn, w["wx_q"], out_dtype=BF16)
    o = _xattn(q.reshape(b, t, -1), mk, mv)
    return _matmul(o.reshape(b * t, -1), w["wx_o"], res=x1, tn=1024)


def kernel(x_prompt, x_sample, mem_prompt, cache_moba_k, cache_moba_v, page_table, cache_mem_k, cache_mem_v, state_rwkv_wkv, state_rwkv_shift, state_ffn_conv, norm_mix, w_in, rw_mu, rw_w0, rw_w2, rw_a0, rw_a2, rw_g2, rw_k_k, rw_k_a, rw_r_k, rw_ln_w, rw_ln_b, w_a_out, w_b_out, w_o, norm_x, norm_mem, wx_q, wx_k, wx_v, wx_o, norm_ffn, w_gate, w_up, conv_w, conv_b, w_down, norm_out):
    assert w_in.shape[0] == 1, "single-layer model"
    bp, t, d = x_prompt.shape
    db, ts, _ = x_sample.shape
    assert ts == 1, "decode group advances one token"
    rw_cols = state_rwkv_shift.shape[-1]
    lay = _layout(d, rw_cols)
    rw, mbw = lay["rw"], lay["mb"]
    nh_rw, nh_mb = rw // RW_HEAD, mbw // MB_HEAD
    f = w_gate.shape[-1]
    n_mem = mem_prompt.shape[1]
    xw = wx_q.shape[-1]
    n_pages = page_table.shape[1]
    page = cache_moba_k.shape[2]
    assert page == PAGE_SIZE and n_pages % (MB_BLOCK // page) == 0
    past_len = n_pages * page

    bf = lambda z: z[0].astype(BF16)
    w = dict(w_a_out=w_a_out[0], w_b_out=w_b_out[0], w_o=bf(w_o), norm_x=norm_x[0],
             wx_q=bf(wx_q), wx_o=bf(wx_o))
    prm = _rwkv_params(lay, rw_mu[0], rw_w0[0], rw_w2[0], rw_a0[0], rw_a2[0], rw_g2[0],
                       rw_k_k[0], rw_k_a[0], rw_r_k[0], rw_ln_w[0], rw_ln_b[0])
    w_down_b = bf(w_down)
    w_in_t = jnp.swapaxes(w_in[0], 0, 1)

    xp = x_prompt.reshape(bp * t, d)
    xs = x_sample.reshape(db, d)
    proj_p, proj_s = _project(_rmsnorm(xp, norm_mix[0], BF16), _rmsnorm(xs, norm_mix[0], BF16),
                              w_in_t, lay)
    proj_p3 = proj_p.reshape(bp, t, lay["np"])
    zeros_prev = [jnp.zeros((bp, 1, wd), F32) for wd in (rw, rw, rw, lay["lp"])]
    streams = _rwkv_prep(proj_p3, zeros_prev, lay, prm)
    o_a_p, sw_p, page_sums = _rwkv_scan(
        streams, jnp.zeros((bp, nh_rw, RW_HEAD, RW_HEAD), F32), prm, min(SCAN_CHUNK, t),
        paged_keys=(cache_moba_k[0], page_table.reshape(-1)))
    k_rot_p, v_p, o_b_p = _moba_prompt(proj_p3, lay)
    x1_p = _mix_and_project(xp, proj_p, o_a_p.reshape(bp * t, rw), o_b_p.reshape(bp * t, mbw), w, lay)

    streams_s = _rwkv_prep(proj_s.reshape(db, 1, lay["np"]), _split_prev(state_rwkv_shift[0], lay),
                           lay, prm)
    streams_s = [jnp.pad(s, ((0, 0), (0, STEP_CHUNK - 1), (0, 0))) for s in streams_s]
    o_a_s, sw_s = _rwkv_scan(streams_s, state_rwkv_wkv[0], prm, STEP_CHUNK)
    o_a_s = o_a_s[:, 0, :]
    q_s, k_rot_s = _rope_rows(proj_s, lay, past_len)
    v_s = proj_s[:, lay["off_mv"]:lay["off_mv"] + mbw]
    heads3 = lambda z: z.reshape(db, nh_mb, MB_HEAD)
    sel = _decode_select(heads3(q_s), page_sums.reshape(db, n_pages, nh_mb, MB_HEAD))
    sel = sel.reshape(db, MB_TOPK, nh_mb).transpose(0, 2, 1)
    o_b_s = _decode_attend(heads3(q_s), heads3(k_rot_s), heads3(v_s), sel, page_table,
                           cache_moba_k[0], cache_moba_v[0])
    x1_s = _mix_and_project(xs, proj_s, o_a_s, o_b_s.astype(BF16), w, lay)

    mem_n = _rmsnorm(mem_prompt.reshape(bp * n_mem, d), norm_mem[0], BF16)
    mk_p = _matmul(mem_n, wx_k[0])
    mv_p = _matmul(mem_n, wx_v[0])
    x2_p, xn3_p = _xblock(x1_p.reshape(bp, t, d), norm_x[0], w["wx_q"], w["wx_o"],
                          mk_p.reshape(bp, n_mem, xw), mv_p.reshape(bp, n_mem, xw), norm_ffn[0])
    x2_p = x2_p.reshape(bp * t, d)
    x2_s = _after_mix(x1_s, cache_mem_k[0].reshape(db, n_mem, xw), cache_mem_v[0].reshape(db, n_mem, xw),
                      db, 1, w)

    conv_prev_s = state_ffn_conv[0]
    hid_p, cv_p, hid_s, gt_s = _ffn_in(
        xn3_p, _rmsnorm(x2_s, norm_ffn[0], BF16), w_gate[0], w_up[0], conv_w[0], conv_b[0],
        jnp.zeros((bp, CONV_W - 1, f), F32), conv_prev_s[:, 0], conv_prev_s[:, 1])
    x3_p = _matmul(hid_p.reshape(bp * t, f), w_down_b, res=x2_p,
                   tm=_pick(bp * t, (512, 256, 128, 64, 32, 16, 8)))
    x3_s = _matmul(hid_s, w_down_b, res=x2_s)
    y_p = _rmsnorm(x3_p, norm_out, F32).reshape(bp, t, d)
    y_s = _rmsnorm(x3_s, norm_out, F32).reshape(db, 1, d)
    cv_s = jnp.stack([conv_prev_s[:, 1], gt_s], axis=1)

    return (y_p, y_s, k_rot_p[None], v_p[None],
            k_rot_s.reshape(1, db, 1, nh_mb, MB_HEAD), v_s.reshape(1, db, 1, nh_mb, MB_HEAD),
            mk_p.reshape(1, bp, n_mem, xw // X_HEAD, X_HEAD), mv_p.reshape(1, bp, n_mem, xw // X_HEAD, X_HEAD),
            sw_p[None], sw_s[None],
            _unpermute_last(proj_p3[:, t - 1], lay)[None], _unpermute_last(proj_s, lay)[None],
            cv_p[None], cv_s[None])
```

```python
import functools
import math

import jax
import jax.numpy as jnp
from jax import lax
from jax.experimental import pallas as pl
from jax.experimental.pallas import tpu as pltpu

F32 = jnp.float32
BF16 = jnp.bfloat16

NORM_EPS = 1e-5
RW_HEAD = 64
RW_GN_EPS = 64e-5
MB_HEAD = 128
MB_BLOCK = 256
MB_TOPK = 3
PAGE_SIZE = 128
ROPE_DIM = MB_HEAD // 4
ROPE_THETA = 500000.0
X_HEAD = 128
CONV_W = 3

LANE = 128
VMEM_LIMIT = 56 * 1024 * 1024
SCAN_CHUNK = 64
SCAN_HEADS = 16
STEP_HEADS = 32
STEP_CHUNK = 8


def _params(sem):
    return pltpu.CompilerParams(dimension_semantics=sem, vmem_limit_bytes=VMEM_LIMIT)


def _pick(n, prefs):
    for p in prefs:
        if n % p == 0:
            return p
    return n


def _dot(a, b, dims=(((1,), (0,)), ((), ())), exact=False):
    if exact:
        return lax.dot_general(a, b, dims, precision=lax.Precision.HIGHEST,
                               preferred_element_type=F32)
    return lax.dot_general(a.astype(BF16), b.astype(BF16), dims, preferred_element_type=F32)


_NT = (((1,), (1,)), ((), ()))
_TN = (((0,), (0,)), ((), ()))


def _sigmoid(x):
    return 1.0 / (1.0 + jnp.exp(-x))


def _rmsnorm_kernel(x_ref, g_ref, o_ref):
    x = x_ref[...]
    y = x * lax.rsqrt(jnp.mean(x * x, axis=-1, keepdims=True) + NORM_EPS)
    o_ref[...] = (y * g_ref[...]).astype(o_ref.dtype)


def _rmsnorm(x, g, out_dtype):
    m, d = x.shape
    tr = _pick(m, (512, 256, 128, 64, 32, 16, 8))
    return pl.pallas_call(
        _rmsnorm_kernel,
        grid=(m // tr,),
        in_specs=[pl.BlockSpec((tr, d), lambda i: (i, 0)),
                  pl.BlockSpec((1, d), lambda i: (0, 0))],
        out_specs=pl.BlockSpec((tr, d), lambda i: (i, 0)),
        out_shape=jax.ShapeDtypeStruct((m, d), out_dtype),
        compiler_params=_params(("parallel",)),
        name="rmsnorm",
    )(x, g.reshape(1, d))


def _mm_kernel(a_ref, w_ref, *refs):
    o_ref = refs[-1]
    out = _dot(a_ref[...], w_ref[...])
    if len(refs) == 2:
        out = out + refs[0][...]
    o_ref[...] = out.astype(o_ref.dtype)


def _matmul(a, w, res=None, out_dtype=F32, tm=None, tn=256):
    m, kdim = a.shape
    n = w.shape[1]
    tm = tm or _pick(m, (1024, 512, 256, 128, 64, 32, 16, 8))
    tn = min(tn, n)
    assert m % tm == 0
    in_specs = [pl.BlockSpec((tm, kdim), lambda i, j: (i, 0)),
                pl.BlockSpec((kdim, tn), lambda i, j: (0, j))]
    args = [a, w]
    if res is not None:
        in_specs.append(pl.BlockSpec((tm, tn), lambda i, j: (i, j)))
        args.append(res)
    return pl.pallas_call(
        _mm_kernel,
        grid=(m // tm, pl.cdiv(n, tn)),
        in_specs=in_specs,
        out_specs=pl.BlockSpec((tm, tn), lambda i, j: (i, j)),
        out_shape=jax.ShapeDtypeStruct((m, n), out_dtype),
        compiler_params=_params(("parallel", "parallel")),
        name="matmul",
    )(*args)


def _proj_kernel(a_ref, as_ref, wt_ref, o_ref, os_ref, wb):
    @pl.when(pl.program_id(1) == 0)
    def _():
        wb[...] = wt_ref[...].astype(BF16)
        os_ref[...] = lax.dot_general(as_ref[...], wb[...], _NT, preferred_element_type=F32)

    o_ref[...] = lax.dot_general(a_ref[...], wb[...], _NT, preferred_element_type=F32)


def _project(xn, xn_s, w_in_t, lay):
    m, kdim = xn.shape
    ms = xn_s.shape[0]
    n_src = w_in_t.shape[0]
    tn, shift = lay["tn"], lay["shift"]
    first_shifted = lay["off_q"] // tn
    tm = _pick(m, (1024, 512, 256, 128, 64, 32, 16, 8))
    sub = 8
    assert lay["np"] % tn == 0 and lay["np"] == n_src + shift and shift % sub == 0
    return pl.pallas_call(
        _proj_kernel,
        grid=(lay["np"] // tn, m // tm),
        in_specs=[pl.BlockSpec((tm, kdim), lambda j, i: (i, 0)),
                  pl.BlockSpec((ms, kdim), lambda j, i: (0, 0)),
                  pl.BlockSpec((pl.Element(tn), pl.Element(kdim)),
                               lambda j, i: ((j * (tn // sub)
                                              - jnp.where(j >= first_shifted, shift // sub, 0)) * sub, 0))],
        out_specs=[pl.BlockSpec((tm, tn), lambda j, i: (i, j)),
                   pl.BlockSpec((ms, tn), lambda j, i: (0, j))],
        out_shape=[jax.ShapeDtypeStruct((m, lay["np"]), F32),
                   jax.ShapeDtypeStruct((ms, lay["np"]), F32)],
        scratch_shapes=[pltpu.VMEM((tn, kdim), BF16)],
        compiler_params=_params(("parallel", "arbitrary")),
        name="in_projection",
    )(xn, xn_s, w_in_t)


def _prep_kernel(zr_ref, zk_ref, zv_ref, zl_ref, pr_ref, pk_ref, pv_ref, pl_ref,
                 mur_ref, muk_ref, muv_ref, mul_ref, w0_ref, a0_ref, kk_ref, ka_ref,
                 w2_ref, a2_ref, g2_ref,
                 r_ref, lw_ref, k_ref, v_ref, kkr_ref, a_ref, g_ref,
                 cr, ck, cv, cl, *, tp):
    t = pl.program_id(1)

    @pl.when(t == 0)
    def _():
        cr[...] = pr_ref[...]
        ck[...] = pk_ref[...]
        cv[...] = pv_ref[...]
        cl[...] = pl_ref[...]

    def shift_mix(z_ref, carry, mu_ref):
        z = z_ref[...]
        if tp == 1:
            prev = carry[...]
        else:
            first = lax.broadcasted_iota(jnp.int32, z.shape, 0) == 0
            prev = jnp.where(first, carry[...], pltpu.roll(z, 1, 0))
        carry[...] = z[tp - 1:tp, :]
        return z + mu_ref[...] * (prev - z)

    r = shift_mix(zr_ref, cr, mur_ref)
    k = shift_mix(zk_ref, ck, muk_ref)
    v = shift_mix(zv_ref, cv, muv_ref)
    xl = shift_mix(zl_ref, cl, mul_ref)
    u = w0_ref[...] + _dot(jnp.tanh(xl), w2_ref[...])
    lw = -math.exp(-0.5) * _sigmoid(u)
    a = _sigmoid(a0_ref[...] + _dot(xl, a2_ref[...]))
    g = _dot(_sigmoid(xl), g2_ref[...])
    r_ref[...] = r
    lw_ref[...] = lw
    k_ref[...] = k * (1.0 + (a - 1.0) * ka_ref[...])
    v_ref[...] = v
    kkr_ref[...] = k * kk_ref[...]
    a_ref[...] = a
    g_ref[...] = g


def _rwkv_prep(proj, prev, lay, prm):
    b, t, _ = proj.shape
    rw, lp = lay["rw"], lay["lp"]
    tp = _pick(t, (128, 64, 32, 16, 8))
    assert lay["off_l"] % lp == 0
    lblk = lay["off_l"] // lp
    row = lambda j: pl.BlockSpec((None, tp, rw), lambda bi, ti, j=j: (bi, ti, j))
    prev_spec = lambda w: pl.BlockSpec((None, 1, w), lambda bi, ti: (bi, 0, 0))
    vec = lambda w: pl.BlockSpec((1, w), lambda bi, ti: (0, 0))
    mat = pl.BlockSpec((lp, rw), lambda bi, ti: (0, 0))
    out_spec = pl.BlockSpec((None, tp, rw), lambda bi, ti: (bi, ti, 0))
    out_sds = jax.ShapeDtypeStruct((b, t, rw), F32)
    return pl.pallas_call(
        functools.partial(_prep_kernel, tp=tp),
        grid=(b, t // tp),
        in_specs=[row(0), row(1), row(2),
                  pl.BlockSpec((None, tp, lp), lambda bi, ti: (bi, ti, lblk)),
                  prev_spec(rw), prev_spec(rw), prev_spec(rw), prev_spec(lp),
                  vec(rw), vec(rw), vec(rw), vec(lp), vec(rw), vec(rw), vec(rw), vec(rw),
                  mat, mat, mat],
        out_specs=[out_spec] * 7,
        out_shape=[out_sds] * 7,
        scratch_shapes=[pltpu.VMEM((1, rw), F32), pltpu.VMEM((1, rw), F32),
                        pltpu.VMEM((1, rw), F32), pltpu.VMEM((1, lp), F32)],
        compiler_params=_params(("parallel", "arbitrary")),
        name="rwkv_prep",
    )(proj, proj, proj, proj, *prev,
      prm["mu_r"], prm["mu_k"], prm["mu_v"], prm["mu_l"], prm["w0"], prm["a0"],
      prm["k_k"], prm["k_a"], prm["w2p"], prm["a2p"], prm["g2p"])


def _scan_kernel(*refs, chunk, heads, n_chunks, n_side):
    if n_side:
        refs = refs[1:]
    (r_ref, lw_ref, k_ref, v_ref, kkr_ref, a_ref, g_ref, rk_ref, lnw_ref, lnb_ref, s0_ref) = refs[:11]
    page_refs = refs[11:11 + n_side]
    o_ref, sf_ref = refs[11 + n_side:13 + n_side]
    s_scr = refs[-1]
    c = pl.program_id(2)

    @pl.when(c == 0)
    def _():
        s_scr[...] = s0_ref[...]

    n = chunk
    row = lax.broadcasted_iota(jnp.int32, (n, n), 0)
    col = lax.broadcasted_iota(jnp.int32, (n, n), 1)
    incl = row >= col
    strict = row > col
    eye = (row == col).astype(F32)

    lw = lw_ref[...]
    cum = _dot(incl.astype(F32), lw, exact=True)
    cum_prev = cum - lw
    total = cum[n - 1:n, :]
    e_cum = jnp.exp(cum)
    e_prev = jnp.exp(cum_prev)
    e_neg = jnp.exp(-cum)
    e_rest = jnp.exp(total - cum)
    e_total = jnp.exp(total)

    r_all, k_all, v_all = r_ref[...], k_ref[...], v_ref[...]
    kkr_all, a_all, g_all = kkr_ref[...], a_ref[...], g_ref[...]
    rk_all, lnw_all, lnb_all = rk_ref[...], lnw_ref[...], lnb_ref[...]

    hs = range(heads)
    sls = [slice(h * RW_HEAD, (h + 1) * RW_HEAD) for h in hs]
    r = [r_all[:, sl] for sl in sls]
    k = [k_all[:, sl] for sl in sls]
    v = [v_all[:, sl] for sl in sls]
    kk, bb, q2 = [], [], []
    for h, sl in enumerate(sls):
        kkr = kkr_all[:, sl]
        kk_h = kkr / jnp.maximum(jnp.sqrt(jnp.sum(kkr * kkr, axis=-1, keepdims=True)), 1e-12)
        kk.append(kk_h)
        bb.append(kk_h * a_all[:, sl])
        q2.append(jnp.concatenate([kk_h * e_prev[:, sl], r[h] * e_cum[:, sl]], axis=0))
    p_k = [_dot(q2[h], k[h] * e_neg[:, sls[h]], _NT) for h in hs]
    p_b = [_dot(q2[h], bb[h] * e_neg[:, sls[h]], _NT) for h in hs]
    s = [s_scr[h] for h in hs]
    qs = [_dot(q2[h], s[h], _NT) for h in hs]
    m_k = [jnp.where(strict, p[:n], 0.0) for p in p_k]
    a_k = [jnp.where(incl, p[n:], 0.0) for p in p_k]
    a_b = [jnp.where(incl, p[n:], 0.0) for p in p_b]
    npow = [jnp.where(strict, -p[:n], 0.0) for p in p_b]
    tinv = [eye + x for x in npow]
    for _ in range(int(math.log2(n)) - 1):
        npow = [_dot(x, x) for x in npow]
        tinv = [t + _dot(t, x) for t, x in zip(tinv, npow)]
    rhs = [qs[h][:n] + _dot(m_k[h], v[h]) for h in hs]
    u = [_dot(tinv[h], rhs[h]) for h in hs]
    y = [qs[h][n:] + _dot(a_k[h], v[h]) - _dot(a_b[h], u[h]) for h in hs]
    for h, sl in enumerate(sls):
        s_scr[h] = (s[h] * e_total[:, sl] + _dot(v[h], k[h] * e_rest[:, sl], _TN)
                    - _dot(u[h], bb[h] * e_rest[:, sl], _TN))
    outs = []
    for h, sl in enumerate(sls):
        mu = jnp.mean(y[h], axis=-1, keepdims=True)
        yc = y[h] - mu
        var = jnp.mean(yc * yc, axis=-1, keepdims=True)
        yn = yc * lax.rsqrt(var + RW_GN_EPS) * lnw_all[:, sl] + lnb_all[:, sl]
        bonus = jnp.sum(r[h] * k[h] * rk_all[:, sl], axis=-1, keepdims=True) * v[h]
        outs.append((yn + bonus) * g_all[:, sl])
    o_ref[...] = jnp.concatenate(outs, axis=1).astype(o_ref.dtype)
    if n_side:
        side_ref = refs[13 + n_side]
        for s in range(n_side):
            side_ref[s] = jnp.sum(page_refs[s][...], axis=0)

    @pl.when(c == n_chunks - 1)
    def _():
        sf_ref[...] = s_scr[...]


def _rwkv_scan(streams, s0, prm, chunk, heads, paged_keys=None):
    b, t, rw = streams[0].shape
    nh = rw // RW_HEAD
    hg = _pick(nh, (heads, 16, 8, 4, 2))
    w = hg * RW_HEAD
    assert t % chunk == 0 and chunk & (chunk - 1) == 0 and w % LANE == 0
    n_chunks, n_groups = t // chunk, nh // hg
    stream_spec = pl.BlockSpec((None, chunk, w), lambda bi, gi, ci, *_: (bi, ci, gi))
    vec_spec = pl.BlockSpec((1, w), lambda bi, gi, ci, *_: (0, gi))
    state_spec = pl.BlockSpec((None, hg, RW_HEAD, RW_HEAD), lambda bi, gi, ci, *_: (bi, gi, 0, 0))
    in_specs = [stream_spec] * 7 + [vec_spec] * 3 + [state_spec]
    out_specs = [stream_spec, state_spec]
    out_shape = [jax.ShapeDtypeStruct((b, t, rw), BF16),
                 jax.ShapeDtypeStruct((b, nh, RW_HEAD, RW_HEAD), F32)]
    args = [*streams, prm["r_k"], prm["ln_w"], prm["ln_b"], s0]
    n_side, n_prefetch = 0, 0
    if paged_keys is not None:
        cache_k, page_ids = paged_keys
        _, page, nhk, d = cache_k.shape
        n_ids = page_ids.shape[0]
        n_steps = b * n_groups * n_chunks
        n_side = pl.cdiv(n_ids, n_steps)
        step = lambda bi, gi, ci: (bi * n_groups + gi) * n_chunks + ci

        def page_spec(s):
            return pl.BlockSpec(
                (None, page, nhk, d),
                lambda bi, gi, ci, ids: (ids[jnp.minimum(step(bi, gi, ci) * n_side + s, n_ids - 1)], 0, 0, 0))

        in_specs += [page_spec(s) for s in range(n_side)]
        out_specs.append(pl.BlockSpec((n_side, nhk, d), lambda bi, gi, ci, ids: (step(bi, gi, ci), 0, 0)))
        out_shape.append(jax.ShapeDtypeStruct((n_steps * n_side, nhk, d), F32))
        args = [page_ids] + args + [cache_k] * n_side
        n_prefetch = 1
    outs = pl.pallas_call(
        functools.partial(_scan_kernel, chunk=chunk, heads=hg, n_chunks=n_chunks, n_side=n_side),
        grid_spec=pltpu.PrefetchScalarGridSpec(
            num_scalar_prefetch=n_prefetch,
            grid=(b, n_groups, n_chunks),
            in_specs=in_specs,
            out_specs=out_specs,
            scratch_shapes=[pltpu.VMEM((hg, RW_HEAD, RW_HEAD), F32)],
        ),
        out_shape=out_shape,
        compiler_params=_params(("parallel", "parallel", "arbitrary")),
        name="rwkv_scan",
    )(*args)
    if paged_keys is not None:
        return outs[0], outs[1], outs[2][:n_ids]
    return outs[0], outs[1]


def _rope_tables(pos):
    half = ROPE_DIM // 2
    inv = 1.0 / (ROPE_THETA ** (jnp.arange(0, ROPE_DIM, 2, dtype=F32) / ROPE_DIM))
    ang = pos.astype(F32)[:, None] * inv[None, :]
    cos, sin = jnp.cos(ang), jnp.sin(ang)
    t = pos.shape[0]
    rest = MB_HEAD - ROPE_DIM
    c = jnp.concatenate([cos, cos, jnp.ones((t, rest), F32)], axis=1)
    s = jnp.concatenate([-sin, sin, jnp.zeros((t, rest), F32)], axis=1)
    del half
    return c, s


def _rope(x, c, s):
    half = ROPE_DIM // 2
    lane = lax.broadcasted_iota(jnp.int32, x.shape, 1)
    swapped = jnp.where(lane < half, pltpu.roll(x, MB_HEAD - half, 1), pltpu.roll(x, half, 1))
    return x * c + swapped * s


def _moba_prompt_kernel(q_ref, k_ref, v_ref, c_ref, s_ref, hot_ref, kout_hbm, vout_hbm, o_ref,
                        kbuf, vbuf, sems, *, nb):
    bi, hi = pl.program_id(0), pl.program_id(1)
    c, s = c_ref[...], s_ref[...]
    q = _rope(q_ref[...], c, s)
    k = _rope(k_ref[...], c, s)
    kbuf[...] = k
    vbuf[...] = v_ref[...]
    k_copy = pltpu.make_async_copy(kbuf, kout_hbm.at[bi, :, hi, :], sems.at[0])
    v_copy = pltpu.make_async_copy(vbuf, vout_hbm.at[bi, :, hi, :], sems.at[1])
    k_copy.start()
    v_copy.start()
    t = q.shape[0]
    k_means = jnp.sum(k.reshape(nb, MB_BLOCK, MB_HEAD), axis=1) * (1.0 / MB_BLOCK)
    nbp = -(-nb // 8) * 8
    if nbp > nb:
        k_means = jnp.concatenate([k_means, jnp.zeros((nbp - nb, MB_HEAD), F32)], axis=0)
    scores = _dot(k_means, q, _NT, exact=True)
    blk = lax.broadcasted_iota(jnp.int32, (nbp, t), 0)
    tpos = lax.broadcasted_iota(jnp.int32, (nbp, t), 1)
    rank = jnp.zeros((nbp, t), jnp.int32)
    for m in range(nb):
        sm = scores[m:m + 1, :]
        ahead = (sm > scores) | ((sm == scores) & (blk > m))
        rank = rank + jnp.where(ahead & (tpos >= (m + 1) * MB_BLOCK), 1, 0)
    own = (tpos >= blk * MB_BLOCK) & (tpos < (blk + 1) * MB_BLOCK)
    sel = ((rank < MB_TOPK) & (tpos >= (blk + 1) * MB_BLOCK)) | own
    eye = (lax.broadcasted_iota(jnp.int32, (nbp, MB_HEAD), 0)
           == lax.broadcasted_iota(jnp.int32, (nbp, MB_HEAD), 1)).astype(F32)
    sel_rows = _dot(jnp.where(sel, 1.0, 0.0), eye, _TN)
    lane = lax.broadcasted_iota(jnp.int32, (t, MB_HEAD), 1)
    masked = -1e30
    bias = jnp.where((lane < nb) & (sel_rows < 0.5), masked, 0.0)
    scale = MB_HEAD ** -0.5 * math.log2(math.e)
    qa = jnp.concatenate([(q * scale).astype(BF16), bias.astype(BF16)], axis=1)
    ka = jnp.concatenate([k.astype(BF16), hot_ref[...]], axis=1)
    vb = v_ref[...].astype(BF16)
    rowi = lax.broadcasted_iota(jnp.int32, (MB_BLOCK, MB_BLOCK), 0)
    coli = lax.broadcasted_iota(jnp.int32, (MB_BLOCK, MB_BLOCK), 1)
    causal = rowi >= coli
    for i in range(nb):
        rows = slice(i * MB_BLOCK, (i + 1) * MB_BLOCK)
        q_i = qa[rows]
        logits = [_dot(q_i, ka[j * MB_BLOCK:(j + 1) * MB_BLOCK], _NT) for j in range(i)]
        logits.append(jnp.where(causal, _dot(q_i, ka[rows], _NT), masked))
        top = logits[0]
        for lg in logits[1:]:
            top = jnp.maximum(top, lg)
        mx = top.max(axis=-1, keepdims=True)
        p_sum = jnp.zeros((MB_BLOCK, MB_BLOCK), F32)
        acc = jnp.zeros((MB_BLOCK, MB_HEAD), F32)
        for j, lg in enumerate(logits):
            pj = jnp.exp2(lg - mx)
            p_sum = p_sum + pj
            acc = acc + _dot(pj, vb[j * MB_BLOCK:(j + 1) * MB_BLOCK])
        o_ref[rows, :] = (acc / jnp.sum(p_sum, axis=-1, keepdims=True)).astype(o_ref.dtype)
    k_copy.wait()
    v_copy.wait()


def _moba_prompt(proj, lay):
    b, t, _ = proj.shape
    assert t % MB_BLOCK == 0
    mbw = lay["mb"]
    nh = mbw // MB_HEAD
    assert t // MB_BLOCK <= MB_HEAD
    ctab, stab = _rope_tables(jnp.arange(t))
    block_hot = (jnp.arange(t)[:, None] // MB_BLOCK == jnp.arange(MB_HEAD)[None, :]).astype(BF16)
    qb, kb, vb = lay["off_q"] // MB_HEAD, lay["off_mk"] // MB_HEAD, lay["off_mv"] // MB_HEAD
    head = lambda off: pl.BlockSpec((None, t, MB_HEAD), lambda bi, hi, off=off: (bi, 0, off + hi))
    tab = pl.BlockSpec((t, MB_HEAD), lambda bi, hi: (0, 0))
    hbm = pl.BlockSpec(memory_space=pl.ANY)
    rows_sds = jax.ShapeDtypeStruct((b, t, nh, MB_HEAD), F32)
    return pl.pallas_call(
        functools.partial(_moba_prompt_kernel, nb=t // MB_BLOCK),
        grid=(b, nh),
        in_specs=[head(qb), head(kb), head(vb), tab, tab, tab],
        out_specs=[hbm, hbm, pl.BlockSpec((None, t, MB_HEAD), lambda bi, hi: (bi, 0, hi))],
        out_shape=[rows_sds, rows_sds, jax.ShapeDtypeStruct((b, t, mbw), BF16)],
        scratch_shapes=[pltpu.VMEM((t, MB_HEAD), F32), pltpu.VMEM((t, MB_HEAD), F32),
                        pltpu.SemaphoreType.DMA((2,))],
        compiler_params=_params(("parallel", "parallel")),
        name="moba_prompt",
    )(proj, proj, proj, ctab, stab, block_hot)


def _rope_rows_kernel(q_ref, k_ref, c_ref, s_ref, qo_ref, ko_ref, *, heads):
    c, s = c_ref[...], s_ref[...]
    for h in range(heads):
        sl = slice(h * MB_HEAD, (h + 1) * MB_HEAD)
        qo_ref[:, sl] = _rope(q_ref[:, sl], c, s)
        ko_ref[:, sl] = _rope(k_ref[:, sl], c, s)


def _rope_rows(proj2d, lay, pos):
    m = proj2d.shape[0]
    mbw = lay["mb"]
    ctab, stab = _rope_tables(jnp.full((1,), pos))
    cw = math.gcd(math.gcd(lay["off_q"], lay["off_mk"]), mbw)
    assert cw % MB_HEAD == 0
    spec = lambda off: pl.BlockSpec((m, cw), lambda i, off=off: (0, off // cw + i))
    tab = pl.BlockSpec((1, MB_HEAD), lambda i: (0, 0))
    out = pl.BlockSpec((m, cw), lambda i: (0, i))
    sds = jax.ShapeDtypeStruct((m, mbw), F32)
    return pl.pallas_call(
        functools.partial(_rope_rows_kernel, heads=cw // MB_HEAD),
        grid=(mbw // cw,),
        in_specs=[spec(lay["off_q"]), spec(lay["off_mk"]), tab, tab],
        out_specs=[out, out],
        out_shape=[sds, sds],
        compiler_params=_params(("arbitrary",)),
        name="rope_rows",
    )(proj2d, proj2d, ctab, stab)


def _decode_select_kernel(q_ref, ps_ref, idx_ref, *, nb, ppb):
    q = q_ref[...]
    page_sums = ps_ref[...]
    block_sums = jnp.sum(page_sums.reshape((nb, ppb) + page_sums.shape[1:]), axis=1)
    scores = jnp.sum(block_sums * (1.0 / MB_BLOCK) * q[None], axis=-1, keepdims=True)
    blk = lax.broadcasted_iota(jnp.int32, scores.shape, 0)
    rank = jnp.zeros(scores.shape, jnp.int32)
    for m in range(nb):
        sm = scores[m:m + 1]
        ahead = (sm > scores) | ((sm == scores) & (blk > m))
        rank = rank + ahead.astype(jnp.int32)
    for r in range(MB_TOPK):
        idx_ref[r] = jnp.sum(jnp.where(rank == r, blk, 0), axis=0)


def _decode_select(q, page_sums):
    db, nh, d = q.shape
    n_pages = page_sums.shape[1]
    ppb = MB_BLOCK // PAGE_SIZE
    nb = n_pages // ppb
    assert nb >= MB_TOPK
    return pl.pallas_call(
        functools.partial(_decode_select_kernel, nb=nb, ppb=ppb),
        grid=(db,),
        in_specs=[pl.BlockSpec((None, nh, d), lambda bi: (bi, 0, 0)),
                  pl.BlockSpec((None, n_pages, nh, d), lambda bi: (bi, 0, 0, 0))],
        out_specs=pl.BlockSpec((None, MB_TOPK, nh, 1), lambda bi: (bi, 0, 0, 0)),
        out_shape=jax.ShapeDtypeStruct((db, MB_TOPK, nh, 1), jnp.int32),
        compiler_params=_params(("parallel",)),
        name="moba_decode_select",
    )(q, page_sums)


def _decode_attend_kernel(sel_ref, pt_ref, q_ref, kn_ref, vn_ref, ck_hbm, cv_hbm, o_ref,
                          kbuf, vbuf, sems, *, nh, ppb):
    bi = pl.program_id(0)
    n_sel = MB_TOPK * ppb
    page = kbuf.shape[2]

    def head_copies(h):
        copies = []
        for r in range(MB_TOPK):
            blk = sel_ref[(bi * nh + h) * MB_TOPK + r]
            for j in range(ppb):
                phys = pt_ref[bi, blk * ppb + j]
                s = r * ppb + j
                copies.append(pltpu.make_async_copy(ck_hbm.at[phys, :, h, :], kbuf.at[h, s], sems.at[h]))
                copies.append(pltpu.make_async_copy(cv_hbm.at[phys, :, h, :], vbuf.at[h, s], sems.at[h]))
        return copies

    all_copies = [head_copies(h) for h in range(nh)]
    for copies in all_copies:
        for cp in copies:
            cp.start()

    scale = MB_HEAD ** -0.5
    for h in range(nh):
        for cp in all_copies[h]:
            cp.wait()
        q = q_ref[h:h + 1, :]
        keys = kbuf[h].reshape(n_sel * page, MB_HEAD)
        vals = vbuf[h].reshape(n_sel * page, MB_HEAD)
        lg = _dot(jnp.broadcast_to(q, (8, MB_HEAD)), keys, _NT)[0:1] * scale
        own = jnp.sum(q * kn_ref[h:h + 1, :], axis=-1, keepdims=True) * scale
        mx = jnp.maximum(own, lg.max(axis=-1, keepdims=True))
        p_own = jnp.exp(own - mx)
        p = jnp.exp(lg - mx)
        den = p_own + jnp.sum(p, axis=-1, keepdims=True)
        acc = p_own * vn_ref[h:h + 1, :] + _dot(jnp.broadcast_to(p, (8, p.shape[1])), vals)[0:1]
        o_ref[h:h + 1, :] = acc / den


def _decode_attend(q, k_new, v_new, sel, page_table, cache_k, cache_v):
    db, nh, d = q.shape
    page = cache_k.shape[1]
    ppb = MB_BLOCK // page
    n_sel = MB_TOPK * ppb
    vec = pl.BlockSpec((None, nh, d), lambda bi, sel_r, pt_r: (bi, 0, 0))
    hbm = pl.BlockSpec(memory_space=pl.ANY)
    out = pl.pallas_call(
        functools.partial(_decode_attend_kernel, nh=nh, ppb=ppb),
        grid_spec=pltpu.PrefetchScalarGridSpec(
            num_scalar_prefetch=2,
            grid=(db,),
            in_specs=[vec, vec, vec, hbm, hbm],
            out_specs=vec,
            scratch_shapes=[pltpu.VMEM((nh, n_sel, page, d), F32),
                            pltpu.VMEM((nh, n_sel, page, d), F32),
                            pltpu.SemaphoreType.DMA((nh,))],
        ),
        out_shape=jax.ShapeDtypeStruct((db, nh, d), F32),
        compiler_params=_params(("arbitrary",)),
        name="moba_decode_attend",
    )(sel.reshape(-1), page_table, q, k_new, v_new, cache_k, cache_v)
    return out.reshape(db, nh * d)


def _merge_kernel(oa_ref, ob_ref, wa_ref, wb_ref, ga_ref, gb_ref, o_ref):
    ya = _dot(oa_ref[...], wa_ref[...])
    yb = _dot(ob_ref[...], wb_ref[...])
    o_ref[...] = (_sigmoid(ga_ref[...]) * ya + _sigmoid(gb_ref[...]) * yb).astype(o_ref.dtype)


def _merge(o_a, o_b, w_a, w_b, proj2d, lay):
    m, ka = o_a.shape
    kb = o_b.shape[1]
    d = w_a.shape[1]
    tm = _pick(m, (1024, 512, 256, 128, 64, 32, 16, 8))
    tn = next(c for c in (512, 256, LANE)
              if d % c == 0 and lay["off_ga"] % c == 0 and lay["off_gb"] % c == 0)
    ga, gb = lay["off_ga"] // tn, lay["off_gb"] // tn
    return pl.pallas_call(
        _merge_kernel,
        grid=(m // tm, d // tn),
        in_specs=[pl.BlockSpec((tm, ka), lambda i, j: (i, 0)),
                  pl.BlockSpec((tm, kb), lambda i, j: (i, 0)),
                  pl.BlockSpec((ka, tn), lambda i, j: (0, j)),
                  pl.BlockSpec((kb, tn), lambda i, j: (0, j)),
                  pl.BlockSpec((tm, tn), lambda i, j: (i, ga + j)),
                  pl.BlockSpec((tm, tn), lambda i, j: (i, gb + j))],
        out_specs=pl.BlockSpec((tm, tn), lambda i, j: (i, j)),
        out_shape=jax.ShapeDtypeStruct((m, d), BF16),
        compiler_params=_params(("parallel", "parallel")),
        name="gated_merge",
    )(o_a, o_b, w_a, w_b, proj2d, proj2d)


def _xattn_kernel(q_ref, k_ref, v_ref, o_ref, *, heads):
    scale = X_HEAD ** -0.5
    for h in range(heads):
        sl = slice(h * X_HEAD, (h + 1) * X_HEAD)
        q = q_ref[:, sl]
        if q.shape[0] < 8:
            q = jnp.broadcast_to(q, (8, X_HEAD))
        lg = _dot(q, k_ref[:, sl], _NT) * scale
        p = jnp.exp(lg - lg.max(axis=-1, keepdims=True))
        o = _dot(p, v_ref[:, sl]) / jnp.sum(p, axis=-1, keepdims=True)
        o_ref[:, sl] = o[:o_ref.shape[0]].astype(o_ref.dtype)


def _xattn(q, mk, mv):
    b, t, xw = q.shape
    n_mem = mk.shape[1]
    tq = _pick(t, (512, 256, 128, 64, 32, 16, 8))
    qspec = pl.BlockSpec((None, tq, xw), lambda bi, ti: (bi, ti, 0))
    mspec = pl.BlockSpec((None, n_mem, xw), lambda bi, ti: (bi, 0, 0))
    return pl.pallas_call(
        functools.partial(_xattn_kernel, heads=xw // X_HEAD),
        grid=(b, t // tq),
        in_specs=[qspec, mspec, mspec],
        out_specs=qspec,
        out_shape=jax.ShapeDtypeStruct((b, t, xw), BF16),
        compiler_params=_params(("parallel", "parallel")),
        name="cross_attention",
    )(q, mk, mv)


def _xblock_kernel(x_ref, gx_ref, wq_ref, wo_ref, k_ref, v_ref, gf_ref, x2_ref, xn_ref, *, heads):
    x = x_ref[...]
    xn = x * lax.rsqrt(jnp.mean(x * x, axis=-1, keepdims=True) + NORM_EPS) * gx_ref[...]
    q = _dot(xn, wq_ref[...])
    scale = X_HEAD ** -0.5
    outs = []
    for h in range(heads):
        sl = slice(h * X_HEAD, (h + 1) * X_HEAD)
        lg = _dot(q[:, sl], k_ref[:, sl], _NT) * scale
        p = jnp.exp(lg - lg.max(axis=-1, keepdims=True))
        outs.append(_dot(p, v_ref[:, sl]) / jnp.sum(p, axis=-1, keepdims=True))
    x2 = x + _dot(jnp.concatenate(outs, axis=1), wo_ref[...])
    x2_ref[...] = x2
    xn_ref[...] = (x2 * lax.rsqrt(jnp.mean(x2 * x2, axis=-1, keepdims=True) + NORM_EPS)
                   * gf_ref[...]).astype(xn_ref.dtype)


def _xblock(x, g_x, wq, wo, mk, mv, g_next):
    b, t, d = x.shape
    xw = wq.shape[1]
    n_mem = mk.shape[1]
    tq = _pick(t, (256, 128, 64, 32, 16, 8))
    rows = pl.BlockSpec((None, tq, d), lambda bi, ti: (bi, ti, 0))
    vec = pl.BlockSpec((1, d), lambda bi, ti: (0, 0))
    mem = pl.BlockSpec((None, n_mem, xw), lambda bi, ti: (bi, 0, 0))
    return pl.pallas_call(
        functools.partial(_xblock_kernel, heads=xw // X_HEAD),
        grid=(b, t // tq),
        in_specs=[rows, vec, pl.BlockSpec((d, xw), lambda bi, ti: (0, 0)),
                  pl.BlockSpec((xw, d), lambda bi, ti: (0, 0)), mem, mem, vec],
        out_specs=[rows, rows],
        out_shape=[jax.ShapeDtypeStruct((b, t, d), F32), jax.ShapeDtypeStruct((b, t, d), BF16)],
        compiler_params=_params(("parallel", "parallel")),
        name="cross_attention_block",
    )(x, g_x.reshape(1, d), wq, wo, mk, mv, g_next.reshape(1, d))


def _ffn_in_kernel(x_ref, xs_ref, wg_ref, wu_ref, cw_ref, cb_ref, prev_ref, p0_ref, p1_ref,
                   h_ref, cn_ref, hs_ref, gs_ref, carry, wg_b, wu_b, *, tm, n_t):
    t = pl.program_id(2)

    @pl.when((pl.program_id(1) == 0) & (t == 0))
    def _():
        wg_b[...] = wg_ref[...].astype(BF16)
        wu_b[...] = wu_ref[...].astype(BF16)
        xs = xs_ref[...]
        gs = _dot(xs, wg_b[...])
        acc_s = (cb_ref[...] + p0_ref[...] * cw_ref[0:1, :] + p1_ref[...] * cw_ref[1:2, :]
                 + gs * cw_ref[2:3, :])
        hs_ref[...] = (acc_s * _sigmoid(acc_s) * _dot(xs, wu_b[...])).astype(hs_ref.dtype)
        gs_ref[...] = gs

    @pl.when(t == 0)
    def _():
        carry[...] = prev_ref[...]

    x = x_ref[...]
    gt = _dot(x, wg_b[...])
    up = _dot(x, wu_b[...])
    rowi = lax.broadcasted_iota(jnp.int32, gt.shape, 0)
    c0, c1 = carry[0:1, :], carry[1:2, :]
    g1 = jnp.where(rowi == 0, c1, pltpu.roll(gt, 1, 0))
    g2 = jnp.where(rowi == 0, c0, jnp.where(rowi == 1, c1, pltpu.roll(gt, 2, 0)))
    acc = cb_ref[...] + g2 * cw_ref[0:1, :] + g1 * cw_ref[1:2, :] + gt * cw_ref[2:3, :]
    h_ref[...] = (acc * _sigmoid(acc) * up).astype(h_ref.dtype)
    carry[...] = gt[tm - 2:tm, :]

    @pl.when(t == n_t - 1)
    def _():
        cn_ref[...] = gt[tm - 2:tm, :]


def _ffn_in(xn, xn_s, w_gate, w_up, conv_w, conv_b, conv_prev, prev0_s, prev1_s):
    b, t, d = xn.shape
    ms = xn_s.shape[0]
    f = w_gate.shape[1]
    tm = _pick(t, (1024, 512, 256, 128, 64, 32, 16, 8))
    tn = _pick(f, (256, 128))
    assert tm >= CONV_W - 1
    n_t = t // tm
    wspec = pl.BlockSpec((d, tn), lambda j, bi, ti: (0, j))
    srows = pl.BlockSpec((ms, tn), lambda j, bi, ti: (0, j))
    return pl.pallas_call(
        functools.partial(_ffn_in_kernel, tm=tm, n_t=n_t),
        grid=(f // tn, b, n_t),
        in_specs=[pl.BlockSpec((None, tm, d), lambda j, bi, ti: (bi, ti, 0)),
                  pl.BlockSpec((ms, d), lambda j, bi, ti: (0, 0)),
                  wspec, wspec,
                  pl.BlockSpec((CONV_W, tn), lambda j, bi, ti: (0, j)),
                  pl.BlockSpec((1, tn), lambda j, bi, ti: (0, j)),
                  pl.BlockSpec((None, CONV_W - 1, tn), lambda j, bi, ti: (bi, 0, j)),
                  srows, srows],
        out_specs=[pl.BlockSpec((None, tm, tn), lambda j, bi, ti: (bi, ti, j)),
                   pl.BlockSpec((None, CONV_W - 1, tn), lambda j, bi, ti: (bi, 0, j)),
                   srows, srows],
        out_shape=[jax.ShapeDtypeStruct((b, t, f), BF16),
                   jax.ShapeDtypeStruct((b, CONV_W - 1, f), F32),
                   jax.ShapeDtypeStruct((ms, f), BF16),
                   jax.ShapeDtypeStruct((ms, f), F32)],
        scratch_shapes=[pltpu.VMEM((CONV_W - 1, tn), F32),
                        pltpu.VMEM((d, tn), BF16), pltpu.VMEM((d, tn), BF16)],
        compiler_params=_params(("parallel", "arbitrary", "arbitrary")),
        name="ffn_in",
    )(xn, xn_s, w_gate, w_up, conv_w, conv_b.reshape(1, f), conv_prev, prev0_s, prev1_s)


def _layout(d_model, rw_cols):
    rw = d_model // 2
    mb = d_model // 2
    lora = rw_cols - 3 * rw
    tn = next(c for c in (512, 256, LANE) if (3 * rw) % c == 0 and (-lora) % c <= LANE)
    shift = (-lora) % tn
    lp = lora + shift
    off_l = 3 * rw
    off_q = off_l + lp
    off_ga = off_q + 3 * mb
    return dict(rw=rw, mb=mb, lora=lora, lp=lp, tn=tn, shift=shift, off_l=off_l, off_q=off_q,
                off_mk=off_q + mb, off_mv=off_q + 2 * mb, off_ga=off_ga, off_gb=off_ga + d_model,
                np=off_ga + 2 * d_model)


def _split_prev(prev, lay):
    rw, lora, lp = lay["rw"], lay["lora"], lay["lp"]
    b = prev.shape[0]
    parts = [prev[:, i * rw:(i + 1) * rw] for i in range(3)]
    parts.append(jnp.pad(prev[:, 3 * rw:3 * rw + lora], ((0, 0), (0, lp - lora))))
    return [p.reshape(b, 1, -1) for p in parts]


def _unpermute_last(proj_last, lay):
    rw3, lora = 3 * lay["rw"], lay["lora"]
    return jnp.concatenate([proj_last[:, :rw3], proj_last[:, lay["off_l"]:lay["off_l"] + lora]], axis=-1)


def _rwkv_params(lay, rw_mu, rw_w0, rw_w2, rw_a0, rw_a2, rw_g2, rw_k_k, rw_k_a, rw_r_k, rw_ln_w, rw_ln_b):
    rw, lora, lp = lay["rw"], lay["lora"], lay["lp"]
    dw, da = rw_w2.shape[0], rw_a2.shape[0]
    row = lambda x: x.reshape(1, -1)

    def padded(w, start):
        return jnp.zeros((lp, rw), BF16).at[start:start + w.shape[0]].set(w.astype(BF16))

    return dict(
        mu_r=row(rw_mu[:rw]), mu_k=row(rw_mu[rw:2 * rw]), mu_v=row(rw_mu[2 * rw:3 * rw]),
        mu_l=row(jnp.pad(rw_mu[3 * rw:], (0, lp - lora))),
        w0=row(rw_w0), a0=row(rw_a0), k_k=row(rw_k_k), k_a=row(rw_k_a),
        w2p=padded(rw_w2, 0), a2p=padded(rw_a2, dw), g2p=padded(rw_g2, dw + da),
        r_k=row(rw_r_k), ln_w=row(rw_ln_w), ln_b=row(rw_ln_b))


def _mix_and_project(x2d, proj2d, o_a, o_b, w, lay):
    merged = _merge(o_a, o_b, w["w_a_out"], w["w_b_out"], proj2d, lay)
    return _matmul(merged, w["w_o"], res=x2d, tn=1024)


def _after_mix(x1, mk, mv, b, t, w):
    xn = _rmsnorm(x1, w["norm_x"], BF16)
    q = _matmul(xn, w["wx_q"], out_dtype=BF16)
    o = _xattn(q.reshape(b, t, -1), mk, mv)
    return _matmul(o.reshape(b * t, -1), w["wx_o"], res=x1, tn=1024)


def kernel(x_prompt, x_sample, mem_prompt, cache_moba_k, cache_moba_v, page_table, cache_mem_k, cache_mem_v, state_rwkv_wkv, state_rwkv_shift, state_ffn_conv, norm_mix, w_in, rw_mu, rw_w0, rw_w2, rw_a0, rw_a2, rw_g2, rw_k_k, rw_k_a, rw_r_k, rw_ln_w, rw_ln_b, w_a_out, w_b_out, w_o, norm_x, norm_mem, wx_q, wx_k, wx_v, wx_o, norm_ffn, w_gate, w_up, conv_w, conv_b, w_down, norm_out):
    assert w_in.shape[0] == 1, "single-layer model"
    bp, t, d = x_prompt.shape
    db, ts, _ = x_sample.shape
    assert ts == 1, "decode group advances one token"
    rw_cols = state_rwkv_shift.shape[-1]
    lay = _layout(d, rw_cols)
    rw, mbw = lay["rw"], lay["mb"]
    nh_rw, nh_mb = rw // RW_HEAD, mbw // MB_HEAD
    f = w_gate.shape[-1]
    n_mem = mem_prompt.shape[1]
    xw = wx_q.shape[-1]
    n_pages = page_table.shape[1]
    page = cache_moba_k.shape[2]
    assert page == PAGE_SIZE and n_pages % (MB_BLOCK // page) == 0
    past_len = n_pages * page

    bf = lambda z: z[0].astype(BF16)
    w = dict(w_a_out=w_a_out[0], w_b_out=w_b_out[0], w_o=bf(w_o), norm_x=norm_x[0],
             wx_q=bf(wx_q), wx_o=bf(wx_o))
    prm = _rwkv_params(lay, rw_mu[0], rw_w0[0], rw_w2[0], rw_a0[0], rw_a2[0], rw_g2[0],
                       rw_k_k[0], rw_k_a[0], rw_r_k[0], rw_ln_w[0], rw_ln_b[0])
    w_down_b = bf(w_down)
    w_in_t = jnp.swapaxes(w_in[0], 0, 1)

    xp = x_prompt.reshape(bp * t, d)
    xs = x_sample.reshape(db, d)
    proj_p, proj_s = _project(_rmsnorm(xp, norm_mix[0], BF16), _rmsnorm(xs, norm_mix[0], BF16),
                              w_in_t, lay)
    proj_p3 = proj_p.reshape(bp, t, lay["np"])
    zeros_prev = [jnp.zeros((bp, 1, wd), F32) for wd in (rw, rw, rw, lay["lp"])]
    streams = _rwkv_prep(proj_p3, zeros_prev, lay, prm)
    o_a_p, sw_p, page_sums = _rwkv_scan(
        streams, jnp.zeros((bp, nh_rw, RW_HEAD, RW_HEAD), F32), prm, min(SCAN_CHUNK, t), SCAN_HEADS,
        paged_keys=(cache_moba_k[0], page_table.reshape(-1)))
    k_rot_p, v_p, o_b_p = _moba_prompt(proj_p3, lay)
    x1_p = _mix_and_project(xp, proj_p, o_a_p.reshape(bp * t, rw), o_b_p.reshape(bp * t, mbw), w, lay)

    streams_s = _rwkv_prep(proj_s.reshape(db, 1, lay["np"]), _split_prev(state_rwkv_shift[0], lay),
                           lay, prm)
    streams_s = [jnp.pad(s, ((0, 0), (0, STEP_CHUNK - 1), (0, 0))) for s in streams_s]
    o_a_s, sw_s = _rwkv_scan(streams_s, state_rwkv_wkv[0], prm, STEP_CHUNK, STEP_HEADS)
    o_a_s = o_a_s[:, 0, :]
    q_s, k_rot_s = _rope_rows(proj_s, lay, past_len)
    v_s = proj_s[:, lay["off_mv"]:lay["off_mv"] + mbw]
    heads3 = lambda z: z.reshape(db, nh_mb, MB_HEAD)
    sel = _decode_select(heads3(q_s), page_sums.reshape(db, n_pages, nh_mb, MB_HEAD))
    sel = sel.reshape(db, MB_TOPK, nh_mb).transpose(0, 2, 1)
    o_b_s = _decode_attend(heads3(q_s), heads3(k_rot_s), heads3(v_s), sel, page_table,
                           cache_moba_k[0], cache_moba_v[0])
    x1_s = _mix_and_project(xs, proj_s, o_a_s, o_b_s.astype(BF16), w, lay)

    mem_n = _rmsnorm(mem_prompt.reshape(bp * n_mem, d), norm_mem[0], BF16)
    mk_p = _matmul(mem_n, wx_k[0])
    mv_p = _matmul(mem_n, wx_v[0])
    x2_p, xn3_p = _xblock(x1_p.reshape(bp, t, d), norm_x[0], w["wx_q"], w["wx_o"],
                          mk_p.reshape(bp, n_mem, xw), mv_p.reshape(bp, n_mem, xw), norm_ffn[0])
    x2_p = x2_p.reshape(bp * t, d)
    x2_s = _after_mix(x1_s, cache_mem_k[0].reshape(db, n_mem, xw), cache_mem_v[0].reshape(db, n_mem, xw),
                      db, 1, w)

    conv_prev_s = state_ffn_conv[0]
    hid_p, cv_p, hid_s, gt_s = _ffn_in(
        xn3_p, _rmsnorm(x2_s, norm_ffn[0], BF16), w_gate[0], w_up[0], conv_w[0], conv_b[0],
        jnp.zeros((bp, CONV_W - 1, f), F32), conv_prev_s[:, 0], conv_prev_s[:, 1])
    x3_p = _matmul(hid_p.reshape(bp * t, f), w_down_b, res=x2_p,
                   tm=_pick(bp * t, (512, 256, 128, 64, 32, 16, 8)), tn=512)
    x3_s = _matmul(hid_s, w_down_b, res=x2_s)
    y_p = _rmsnorm(x3_p, norm_out, F32).reshape(bp, t, d)
    y_s = _rmsnorm(x3_s, norm_out, F32).reshape(db, 1, d)
    cv_s = jnp.stack([conv_prev_s[:, 1], gt_s], axis=1)

    return (y_p, y_s, k_rot_p[None], v_p[None],
            k_rot_s.reshape(1, db, 1, nh_mb, MB_HEAD), v_s.reshape(1, db, 1, nh_mb, MB_HEAD),
            mk_p.reshape(1, bp, n_mem, xw // X_HEAD, X_HEAD), mv_p.reshape(1, bp, n_mem, xw // X_HEAD, X_HEAD),
            sw_p[None], sw_s[None],
            _unpermute_last(proj_p3[:, t - 1], lay)[None], _unpermute_last(proj_s, lay)[None],
            cv_p[None], cv_s[None])
```
